```python
import math
import jax, jax.numpy as jnp
from jax import lax
import numpy as np

D_MODEL = 1024
BATCH = 32
SEQ = 2048
DEPTH = 1
DEC_BATCH = 8
DEC_SEQ = 2048
PAST_LEN = 128

HEAD_DIM = 64
N_HEADS_A = 8
N_KV_A = 2
WIN_A = 128
DILATED_GROUPS = ((128, 1), (512, 4), (2048, 16))
HB_PER_GROUP = 4
N_HEADS_B = HB_PER_GROUP * len(DILATED_GROUPS)
N_HEADS_M = 4
HEAD_DIM_M = 128
N_MEM = 256
D_FF = 4 * D_MODEL
NUM_BUCKETS = 32
MAX_DIST = 1024
N_BIAS_HEADS = N_HEADS_A + N_HEADS_B
EPS = 1e-6
NEG = -1e30

W_A = N_HEADS_A * HEAD_DIM
W_KV_A = N_KV_A * HEAD_DIM
W_B = N_HEADS_B * HEAD_DIM
W_B_OUT = HB_PER_GROUP * HEAD_DIM
W_M = N_HEADS_M * HEAD_DIM_M
IN_SPLITS = (W_A, W_KV_A, W_KV_A, W_B, W_B, W_B, W_M, D_MODEL, D_MODEL, D_MODEL)
N_IN = sum(IN_SPLITS)

kernel_name = "hybrid_gated_local_dilated_memory_encoder"


def t5_bucket(rel):
    half = NUM_BUCKETS // 2
    ret = (rel > 0).astype(np.int32) * half
    n = np.abs(rel)
    max_exact = half // 2
    large = max_exact + (np.log(np.maximum(n, 1) / max_exact) / np.log(MAX_DIST / max_exact)
                         * (half - max_exact)).astype(np.int32)
    large = np.minimum(large, half - 1)
    return (ret + np.where(n < max_exact, n, large)).astype(np.int32)


def rms_norm(x, g):
    xf = x.astype(jnp.float32)
    y = xf * lax.rsqrt(jnp.mean(xf * xf, axis=-1, keepdims=True) + EPS)
    return (y * g.astype(jnp.float32)).astype(x.dtype)


def banded_attention(q, k, v, bias_off, half, sink):
    B_, L, H, Dh = q.shape
    G = k.shape[2]
    R = H // G
    W = half
    nb = -(-L // W)
    Lp = nb * W
    pad = Lp - L
    qb = jnp.pad(q, ((0, 0), (0, pad), (0, 0), (0, 0))).reshape(B_, nb, W, G, R, Dh)

    def key_blocks(t):
        tp = jnp.pad(t, ((0, 0), (W, pad + W), (0, 0), (0, 0))).reshape(B_, nb + 2, W, G, Dh)
        return jnp.concatenate([tp[:, :-2], tp[:, 1:-1], tp[:, 2:]], axis=2)

    kb = key_blocks(k)
    vb = key_blocks(v)
    valid = np.pad(np.ones(L, bool), (W, pad + W)).reshape(nb + 2, W)
    valid = np.concatenate([valid[:-2], valid[1:-1], valid[2:]], axis=1)
    off = np.arange(3 * W)[None, :] - W - np.arange(W)[:, None]
    mask = (np.abs(off) <= W)[None] & valid[:, None, :]
    bias = bias_off[:, np.clip(off + W, 0, 2 * W)].astype(jnp.float32).reshape(G, R, W, 3 * W)
    s = jnp.einsum('bnqgrd,bnkgd->bngrqk', qb, kb,
                   preferred_element_type=jnp.float32) * (Dh ** -0.5) + bias
    s = jnp.where(mask[None, :, None, None], s, NEG)
    m = jnp.max(s, axis=-1)
    if sink is not None:
        sk = sink.astype(jnp.float32).reshape(G, R)[None, None, :, :, None]
        m = jnp.maximum(m, sk)
    p = jnp.exp(s - m[..., None])
    den = jnp.sum(p, axis=-1)
    if sink is not None:
        den = den + jnp.exp(sk - m)
    out = jnp.einsum('bngrqk,bnkgd->bnqgrd', p, vb.astype(jnp.float32))
    out = out / jnp.moveaxis(den, -1, 2)[..., None]
    out = out.reshape(B_, Lp, H, Dh)[:, :L].astype(q.dtype)
    lse = jnp.moveaxis(m + jnp.log(den), -1, 2).reshape(B_, Lp, H)[:, :L]
    return out, lse


def dilated_attention(q, k, v, rel_bias):
    B_, S, _, Dh = q.shape
    outs, lses = [], []
    for g, (win, dil) in enumerate(DILATED_GROUPS):
        hs = slice(g * HB_PER_GROUP, (g + 1) * HB_PER_GROUP)
        half = win // (2 * dil)
        Ld = S // dil

        def strided(t):
            return (t[:, :, hs].reshape(B_, Ld, dil, HB_PER_GROUP, Dh)
                    .transpose(0, 2, 1, 3, 4).reshape(B_ * dil, Ld, HB_PER_GROUP, Dh))

        cols = N_HEADS_A + g * HB_PER_GROUP
        rel = dil * np.arange(-half, half + 1)
        bias_off = rel_bias[t5_bucket(rel)][:, cols:cols + HB_PER_GROUP].T
        o, l = banded_attention(strided(q), strided(k), strided(v), bias_off, half, None)
        o = o.reshape(B_, dil, Ld, HB_PER_GROUP, Dh).transpose(0, 2, 1, 3, 4).reshape(B_, S, HB_PER_GROUP, Dh)
        l = l.reshape(B_, dil, Ld, HB_PER_GROUP).transpose(0, 2, 1, 3).reshape(B_, S, HB_PER_GROUP)
        outs.append(o)
        lses.append(l)
    w = jax.nn.softmax(jnp.stack(lses), axis=0)
    out = jnp.sum(w[..., None] * jnp.stack(outs).astype(jnp.float32), axis=0)
    return out.astype(q.dtype)


def memory_attention(q, mk, mv):
    s = jnp.einsum('bshd,bmhd->bhsm', q, mk, preferred_element_type=jnp.float32) * (q.shape[-1] ** -0.5)
    p = jax.nn.softmax(s, axis=-1)
    return jnp.einsum('bhsm,bmhd->bshd', p, mv.astype(jnp.float32)).astype(q.dtype)


def encoder_layer(x, mem, rel_bias, norm1_g, w_in, mem_norm_g, w_mem_kv, sink_logit,
                  w_branch_a, w_branch_b, w_branch_m, w_out, norm2_g, w_up, w_down):
    B_, S, _ = x.shape
    h = rms_norm(x, norm1_g)
    proj = h @ w_in
    qa, ka, va, qb, kb, vb, qm, ga, gb, gm = jnp.split(
        proj, [int(c) for c in np.cumsum(IN_SPLITS)[:-1]], axis=-1)
    bias_a = rel_bias[t5_bucket(np.arange(-WIN_A, WIN_A + 1))][:, :N_HEADS_A].T
    oa, _ = banded_attention(qa.reshape(B_, S, N_HEADS_A, HEAD_DIM),
                             ka.reshape(B_, S, N_KV_A, HEAD_DIM),
                             va.reshape(B_, S, N_KV_A, HEAD_DIM), bias_a, WIN_A, sink_logit)
    ob = dilated_attention(qb.reshape(B_, S, N_HEADS_B, HEAD_DIM),
                           kb.reshape(B_, S, N_HEADS_B, HEAD_DIM),
                           vb.reshape(B_, S, N_HEADS_B, HEAD_DIM), rel_bias)
    mkv = rms_norm(mem, mem_norm_g) @ w_mem_kv
    mk, mv = jnp.split(mkv, 2, axis=-1)
    Mn = mem.shape[1]
    om = memory_attention(qm.reshape(B_, S, N_HEADS_M, HEAD_DIM_M),
                          mk.reshape(B_, Mn, N_HEADS_M, HEAD_DIM_M),
                          mv.reshape(B_, Mn, N_HEADS_M, HEAD_DIM_M))
    merged = (jax.nn.sigmoid(ga) * (oa.reshape(B_, S, W_A) @ w_branch_a)
              + jax.nn.sigmoid(gb) * (ob.reshape(B_, S, W_B_OUT) @ w_branch_b)
              + jax.nn.sigmoid(gm) * (om.reshape(B_, S, W_M) @ w_branch_m))
    x = x + merged @ w_out
    h2 = rms_norm(x, norm2_g)
    x = x + jnp.square(jax.nn.relu(h2 @ w_up)) @ w_down
    return x


def trunk(x, mem, rel_bias, norm1_g, w_in, mem_norm_g, w_mem_kv, sink_logit,
          w_branch_a, w_branch_b, w_branch_m, w_out, norm2_g, w_up, w_down, final_norm_g):
    for layer in range(DEPTH):
        x = encoder_layer(x, mem, rel_bias, norm1_g[layer], w_in[layer], mem_norm_g[layer],
                          w_mem_kv[layer], sink_logit[layer], w_branch_a[layer], w_branch_b[layer],
                          w_branch_m[layer], w_out[layer], norm2_g[layer], w_up[layer], w_down[layer])
    return rms_norm(x, final_norm_g)


def setup_inputs(seed: int = 0) -> dict:
    key = jax.random.key(seed)
    ks = jax.random.split(key, 20)
    f32 = jnp.float32

    def dense(k, shape):
        return jax.random.normal(k, shape, f32) * (shape[-2] ** -0.5)

    def gain(k, shape):
        return 1.0 + 0.1 * jax.random.normal(k, shape, f32)

    return {
        "x_prompt": jax.random.normal(ks[0], (BATCH, SEQ, D_MODEL), f32),
        "x_sample": jax.random.normal(ks[1], (DEC_BATCH, DEC_SEQ, D_MODEL), f32),
        "mem_prompt": jax.random.normal(ks[2], (BATCH, N_MEM, D_MODEL), f32),
        "mem_sample": jax.random.normal(ks[3], (DEC_BATCH, N_MEM, D_MODEL), f32),
        "rel_bias": 0.5 * jax.random.normal(ks[4], (NUM_BUCKETS, N_BIAS_HEADS), f32),
        "norm1_g": gain(ks[5], (DEPTH, D_MODEL)),
        "w_in": dense(ks[6], (DEPTH, D_MODEL, N_IN)),
        "mem_norm_g": gain(ks[7], (DEPTH, D_MODEL)),
        "w_mem_kv": dense(ks[8], (DEPTH, D_MODEL, 2 * W_M)),
        "sink_logit": 0.5 * jax.random.normal(ks[9], (DEPTH, N_HEADS_A), f32),
        "w_branch_a": dense(ks[10], (DEPTH, W_A, D_MODEL)),
        "w_branch_b": dense(ks[11], (DEPTH, W_B_OUT, D_MODEL)),
        "w_branch_m": dense(ks[12], (DEPTH, W_M, D_MODEL)),
        "w_out": dense(ks[13], (DEPTH, D_MODEL, D_MODEL)),
        "norm2_g": gain(ks[14], (DEPTH, D_MODEL)),
        "w_up": dense(ks[15], (DEPTH, D_MODEL, D_FF)),
        "w_down": dense(ks[16], (DEPTH, D_FF, D_MODEL)),
        "final_norm_g": gain(ks[17], (D_MODEL,)),
    }


def reference(x_prompt, x_sample, mem_prompt, mem_sample, rel_bias, norm1_g, w_in, mem_norm_g,
              w_mem_kv, sink_logit, w_branch_a, w_branch_b, w_branch_m, w_out, norm2_g, w_up,
              w_down, final_norm_g):
    y_prompt = trunk(x_prompt, mem_prompt, rel_bias, norm1_g, w_in, mem_norm_g, w_mem_kv, sink_logit,
                     w_branch_a, w_branch_b, w_branch_m, w_out, norm2_g, w_up, w_down, final_norm_g)
    y_sample = trunk(x_sample, mem_sample, rel_bias, norm1_g, w_in, mem_norm_g, w_mem_kv, sink_logit,
                     w_branch_a, w_branch_b, w_branch_m, w_out, norm2_g, w_up, w_down, final_norm_g)
    return (y_prompt, y_sample)
```

```python
import functools

import numpy as np
import jax
import jax.numpy as jnp
from jax import lax
from jax.experimental import pallas as pl
from jax.experimental.pallas import tpu as pltpu

D_MODEL = 1024
HEAD_DIM = 64
N_HEADS_A = 8
N_KV_A = 2
WIN_A = 128
DILATED_GROUPS = ((128, 1), (512, 4), (2048, 16))
HB_PER_GROUP = 4
N_HEADS_B = HB_PER_GROUP * len(DILATED_GROUPS)
N_HEADS_M = 4
HEAD_DIM_M = 128
D_FF = 4 * D_MODEL
NUM_BUCKETS = 32
MAX_DIST = 1024
EPS = 1e-6
NEG = -1e30

W_A = N_HEADS_A * HEAD_DIM
W_KV_A = N_KV_A * HEAD_DIM
W_B = N_HEADS_B * HEAD_DIM
W_B_OUT = HB_PER_GROUP * HEAD_DIM
W_M = N_HEADS_M * HEAD_DIM_M
IN_SPLITS = (W_A, W_KV_A, W_KV_A, W_B, W_B, W_B, W_M, D_MODEL, D_MODEL, D_MODEL)

LANES = 128
QBLK = 128
VMEM_LIMIT = 56 * 1024 * 1024
BF16 = jnp.bfloat16
F32 = jnp.float32


def _t5_bucket(rel):
    half = NUM_BUCKETS // 2
    ret = (rel > 0).astype(np.int32) * half
    n = np.abs(rel)
    max_exact = half // 2
    large = max_exact + (np.log(np.maximum(n, 1) / max_exact) / np.log(MAX_DIST / max_exact)
                         * (half - max_exact)).astype(np.int32)
    large = np.minimum(large, half - 1)
    return (ret + np.where(n < max_exact, n, large)).astype(np.int32)


def _bias_tiles(rel_bias, cols, half, dil, kw, deltas):
    tiles = []
    for delta in deltas:
        off = np.arange(kw)[None, :] - np.arange(QBLK)[:, None] + delta
        valid = np.abs(off) <= half
        bucket = _t5_bucket(dil * np.clip(off, -half, half))
        t = rel_bias[bucket][:, :, cols]
        t = jnp.where(valid[:, :, None], t.astype(F32), NEG)
        tiles.append(jnp.transpose(t, (2, 0, 1)))
    return jnp.stack(tiles)


def _params(sem):
    return pltpu.CompilerParams(dimension_semantics=sem, vmem_limit_bytes=VMEM_LIMIT)


def _rms(x, g):
    return x * lax.rsqrt(jnp.mean(x * x, axis=-1, keepdims=True) + EPS) * g


def _in_proj_kernel(x_ref, g_ref, w_ref, *o_refs, widths):
    h = _rms(x_ref[...], g_ref[...]).astype(BF16)
    c = 0
    for o_ref, wd in zip(o_refs, widths):
        for s in range(0, wd, 512):
            e = min(wd, s + 512)
            o_ref[:, s:e] = jnp.dot(h, w_ref[:, c + s:c + e],
                                    preferred_element_type=F32).astype(o_ref.dtype)
        c += wd


def _in_proj(x2d, g, w, widths, tm):
    t = x2d.shape[0]
    n = w.shape[1]
    return pl.pallas_call(
        functools.partial(_in_proj_kernel, widths=widths),
        grid=(t // tm,),
        in_specs=[pl.BlockSpec((tm, D_MODEL), lambda i: (i, 0)),
                  pl.BlockSpec((1, D_MODEL), lambda i: (0, 0)),
                  pl.BlockSpec((D_MODEL, n), lambda i: (0, 0))],
        out_specs=[pl.BlockSpec((tm, wd), lambda i: (i, 0)) for wd in widths],
        out_shape=[jax.ShapeDtypeStruct((t, wd), BF16) for wd in widths],
        compiler_params=_params(("parallel",)),
        name="in_proj",
    )(x2d, g, w)


def _pair_attention(q2, k2, v2, bias_lo, bias_hi, sink_lo=None, sink_hi=None):
    lane = lax.broadcasted_iota(jnp.int32, q2.shape, 1)
    zero = jnp.zeros_like(q2)
    outs, lses = [], []
    for q_half, bias, sink in ((jnp.where(lane < HEAD_DIM, q2, zero), bias_lo, sink_lo),
                               (jnp.where(lane >= HEAD_DIM, q2, zero), bias_hi, sink_hi)):
        s = lax.dot_general(q_half, k2, (((1,), (1,)), ((), ())),
                            preferred_element_type=F32) + bias
        m = jnp.max(s, axis=-1, keepdims=True)
        if sink is not None:
            m = jnp.maximum(m, sink)
        p = jnp.exp(s - m)
        den = jnp.sum(p, axis=-1, keepdims=True)
        if sink is not None:
            den = den + jnp.exp(sink - m)
        o = jnp.dot(p.astype(BF16), v2, preferred_element_type=F32) / den
        outs.append(o)
        lses.append(m + jnp.log(den))
    lane_o = lax.broadcasted_iota(jnp.int32, outs[0].shape, 1)
    return jnp.where(lane_o < HEAD_DIM, outs[0], outs[1]), lses[0], lses[1]


def _window(n, nblk, half, length, kw):
    if nblk == 1:
        return 0, 0
    ws = jnp.clip(n * QBLK - half, 0, length - kw)
    var = jnp.where(n == 0, 0, jnp.where(n == nblk - 1, 2, 1))
    return pl.multiple_of(ws, HEAD_DIM), var


def _attn_win_kernel(sink_ref, q_ref, kv_ref, bias_ref, o_ref, *, seq):
    n = pl.program_id(1)
    nblk = seq // QBLK
    kw = 3 * WIN_A
    ws, var = _window(n, nblk, WIN_A, seq, kw)
    k2 = kv_ref[pl.ds(ws, kw), 0:LANES]
    v2 = kv_ref[pl.ds(ws, kw), LANES:2 * LANES]
    half_heads = N_HEADS_A // 2
    for t in range(half_heads):
        o, _, _ = _pair_attention(q_ref[:, t * LANES:(t + 1) * LANES], k2, v2,
                                  bias_ref[var, t], bias_ref[var, half_heads + t],
                                  sink_ref[t], sink_ref[half_heads + t])
        o_ref[:, t * LANES:(t + 1) * LANES] = o.astype(o_ref.dtype)


def _attn_win(sink, qa, kva, bias, b, seq):
    nblk = seq // QBLK
    return pl.pallas_call(
        functools.partial(_attn_win_kernel, seq=seq),
        grid=(b, nblk),
        in_specs=[pl.BlockSpec(memory_space=pltpu.SMEM),
                  pl.BlockSpec((QBLK, W_A), lambda i, j: (i * nblk + j, 0)),
                  pl.BlockSpec((None, seq, 2 * LANES), lambda i, j: (i, 0, 0)),
                  pl.BlockSpec(bias.shape, lambda i, j: (0, 0, 0, 0))],
        out_specs=pl.BlockSpec((QBLK, W_A), lambda i, j: (i * nblk + j, 0)),
        out_shape=jax.ShapeDtypeStruct((b * seq, W_A), BF16),
        compiler_params=_params(("parallel", "arbitrary")),
        name="attn_win",
    )(sink, qa, kva.reshape(b, seq, 2 * LANES), bias)


def _attn_dil_kernel(qkv_ref, bias_ref, o_ref, l_ref, *, dil, length, half, kw):
    nblk = length // QBLK

    def body(it, carry):
        r = it // nblk
        n = it % nblk
        ws, var = _window(n, nblk, half, length, kw)
        qs = pl.multiple_of(n * QBLK, QBLK)
        for t in range(HB_PER_GROUP // 2):
            c = t * LANES
            q2 = qkv_ref[r, pl.ds(qs, QBLK), c:c + LANES]
            k2 = qkv_ref[r, pl.ds(ws, kw), W_B_OUT + c:W_B_OUT + c + LANES]
            v2 = qkv_ref[r, pl.ds(ws, kw), 2 * W_B_OUT + c:2 * W_B_OUT + c + LANES]
            o, l_lo, l_hi = _pair_attention(q2, k2, v2, bias_ref[var, 2 * t], bias_ref[var, 2 * t + 1])
            lane = lax.broadcasted_iota(jnp.int32, o.shape, 1)
            o_ref[r, pl.ds(qs, QBLK), c:c + LANES] = o
            l_ref[r, pl.ds(qs, QBLK), c:c + LANES] = jnp.where(lane < HEAD_DIM, l_lo, l_hi)
        return carry

    lax.fori_loop(0, dil * nblk, body, 0)


def _attn_dil(qkv, bias, dil, half):
    b, _, length, _ = qkv.shape
    kw = bias.shape[-1]
    blk = (None, dil, length, W_B_OUT)
    return pl.pallas_call(
        functools.partial(_attn_dil_kernel, dil=dil, length=length, half=half, kw=kw),
        grid=(b,),
        in_specs=[pl.BlockSpec((None, dil, length, 3 * W_B_OUT), lambda i: (i, 0, 0, 0)),
                  pl.BlockSpec(bias.shape, lambda i: (0, 0, 0, 0))],
        out_specs=[pl.BlockSpec(blk, lambda i: (i, 0, 0, 0))] * 2,
        out_shape=[jax.ShapeDtypeStruct((b, dil, length, W_B_OUT), F32)] * 2,
        compiler_params=_params(("parallel",)),
        name=f"attn_dil{dil}",
    )(qkv, bias)


def _attn_mem_kernel(q_ref, mkv_ref, o_ref, *, tq):
    scale = HEAD_DIM_M ** -0.5
    for h in range(N_HEADS_M):
        c = h * HEAD_DIM_M
        mk = mkv_ref[:, c:c + HEAD_DIM_M]
        mv = mkv_ref[:, W_M + c:W_M + c + HEAD_DIM_M]
        for r in range(0, tq, QBLK):
            s = lax.dot_general(q_ref[r:r + QBLK, c:c + HEAD_DIM_M], mk, (((1,), (1,)), ((), ())),
                                preferred_element_type=F32) * scale
            m = jnp.max(s, axis=-1, keepdims=True)
            p = jnp.exp(s - m)
            den = jnp.sum(p, axis=-1, keepdims=True)
            o = jnp.dot(p.astype(BF16), mv, preferred_element_type=F32) / den
            o_ref[r:r + QBLK, c:c + HEAD_DIM_M] = o.astype(o_ref.dtype)


def _attn_mem(qm, mkv, b, seq, n_mem, tq):
    nq = seq // tq
    return pl.pallas_call(
        functools.partial(_attn_mem_kernel, tq=tq),
        grid=(b, nq),
        in_specs=[pl.BlockSpec((tq, W_M), lambda i, j: (i * nq + j, 0)),
                  pl.BlockSpec((None, n_mem, 2 * W_M), lambda i, j: (i, 0, 0))],
        out_specs=pl.BlockSpec((tq, W_M), lambda i, j: (i * nq + j, 0)),
        out_shape=jax.ShapeDtypeStruct((b * seq, W_M), BF16),
        compiler_params=_params(("parallel", "arbitrary")),
        name="attn_mem",
    )(qm, mkv.reshape(b, n_mem, 2 * W_M))


def _tail_kernel(x_ref, oa_ref, om_ref, o0_ref, o1_ref, o2_ref, l0_ref, l1_ref, l2_ref,
                 g1_ref, wg_ref, wa_ref, wb_ref, wm_ref, wo_ref, g2_ref, wu_ref, wd_ref, gf_ref,
                 y_ref, *, final_norm):
    x = x_ref[...]
    h = _rms(x, g1_ref[...]).astype(BF16)

    l0, l1, l2 = l0_ref[...], l1_ref[...], l2_ref[...]
    mx = jnp.maximum(jnp.maximum(l0, l1), l2)
    e0, e1, e2 = jnp.exp(l0 - mx), jnp.exp(l1 - mx), jnp.exp(l2 - mx)
    ob = (e0 * o0_ref[...] + e1 * o1_ref[...] + e2 * o2_ref[...]) / (e0 + e1 + e2)

    def gate(k):
        return jax.nn.sigmoid(jnp.dot(h, wg_ref[:, k * D_MODEL:(k + 1) * D_MODEL],
                                      preferred_element_type=F32))

    merged = gate(0) * jnp.dot(oa_ref[...], wa_ref[...], preferred_element_type=F32)
    merged += gate(1) * jnp.dot(ob.astype(BF16), wb_ref[...], preferred_element_type=F32)
    merged += gate(2) * jnp.dot(om_ref[...], wm_ref[...], preferred_element_type=F32)
    x = x + jnp.dot(merged.astype(BF16), wo_ref[...], preferred_element_type=F32)

    h2 = _rms(x, g2_ref[...]).astype(BF16)
    ck = 1024
    for c in range(0, D_FF, ck):
        u = jnp.dot(h2, wu_ref[:, c:c + ck], preferred_element_type=F32)
        a = jnp.square(jnp.maximum(u, 0.0)).astype(BF16)
        x = x + jnp.dot(a, wd_ref[c:c + ck, :], preferred_element_type=F32)
    if final_norm:
        x = _rms(x, gf_ref[...])
    y_ref[...] = x


def _tail(x2d, oa, om, obs, lbs, g1, wg, wa, wb, wm, wo, g2, wu, wd, gf, tm, final_norm):
    t = x2d.shape[0]

    def rows(width):
        return pl.BlockSpec((tm, width), lambda i: (i, 0))

    def whole(a):
        return pl.BlockSpec(a.shape, lambda i: (0, 0), pipeline_mode=pl.Buffered(1))

    consts = (g1, wg, wa, wb, wm, wo, g2, wu, wd, gf)
    return pl.pallas_call(
        functools.partial(_tail_kernel, final_norm=final_norm),
        grid=(t // tm,),
        in_specs=[rows(D_MODEL), rows(W_A), rows(W_M)] + [rows(W_B_OUT)] * 6 + [whole(a) for a in consts],
        out_specs=rows(D_MODEL),
        out_shape=jax.ShapeDtypeStruct((t, D_MODEL), F32),
        compiler_params=_params(("parallel",)),
        name="tail",
    )(x2d, oa, om, *obs, *lbs, *consts)


def _layer(x, mem, rel_bias, norm1_g, w_in, mem_norm_g, w_mem_kv, sink_logit, w_branch_a,
           w_branch_b, w_branch_m, w_out, norm2_g, w_up, w_down, final_norm_g, final_norm):
    b, seq, _ = x.shape
    n_mem = mem.shape[1]
    t = b * seq
    x2d = x.reshape(t, D_MODEL)

    off = np.cumsum((0,) + IN_SPLITS)
    head_perm = np.arange(N_HEADS_A).reshape(2, N_HEADS_A // 2).T.reshape(-1)
    qa_cols = (off[0] + head_perm[:, None] * HEAD_DIM + np.arange(HEAD_DIM)[None, :]).reshape(-1)
    kva_cols = np.arange(off[1], off[3])
    grp_cols = [np.concatenate([np.arange(off[3 + j] + g * W_B_OUT, off[3 + j] + (g + 1) * W_B_OUT)
                                for j in range(3)]) for g in range(len(DILATED_GROUPS))]
    qm_cols = np.arange(off[6], off[7])
    col_scale = np.ones(off[7], np.float32)
    col_scale[off[0]:off[1]] = HEAD_DIM ** -0.5
    col_scale[off[3]:off[4]] = HEAD_DIM ** -0.5
    cols = np.concatenate([qa_cols, kva_cols] + grp_cols + [qm_cols])
    w_qkv = (w_in[:, cols] * col_scale[cols]).astype(BF16)
    widths = (W_A, 2 * W_KV_A) + (3 * W_B_OUT,) * len(DILATED_GROUPS) + (W_M,)
    qa, kva, qkv0, qkv1, qkv2, qm = _in_proj(x2d, norm1_g.reshape(1, D_MODEL), w_qkv, widths, tm=512)

    bias_a = _bias_tiles(rel_bias, np.arange(N_HEADS_A), WIN_A, 1, 3 * WIN_A, (0, -WIN_A, -2 * WIN_A))
    oa = _attn_win(sink_logit.astype(F32), qa, kva, bias_a, b, seq)

    obs, lbs = [], []
    for g, ((win, dil), qkv) in enumerate(zip(DILATED_GROUPS, (qkv0, qkv1, qkv2))):
        half = win // (2 * dil)
        length = seq // dil
        kw = min(4 * half, length)
        deltas = (0, -half, -2 * half) if length > kw else (0,)
        hcols = N_HEADS_A + g * HB_PER_GROUP + np.arange(HB_PER_GROUP)
        bias_g = _bias_tiles(rel_bias, hcols, half, dil, kw, deltas)
        qkv = qkv.reshape(b, length, dil, 3 * W_B_OUT).transpose(0, 2, 1, 3)
        o, l = _attn_dil(qkv, bias_g, dil, half)
        obs.append(o.transpose(0, 2, 1, 3).reshape(t, W_B_OUT))
        lbs.append(l.transpose(0, 2, 1, 3).reshape(t, W_B_OUT))

    mkv, = _in_proj(mem.reshape(b * n_mem, D_MODEL), mem_norm_g.reshape(1, D_MODEL),
                    w_mem_kv.astype(BF16), (2 * W_M,), tm=512)
    om = _attn_mem(qm, mkv, b, seq, n_mem, tq=512)

    wa = w_branch_a.reshape(N_HEADS_A, HEAD_DIM, D_MODEL)[head_perm].reshape(W_A, D_MODEL)
    y = _tail(x2d, oa, om, obs, lbs, norm1_g.reshape(1, D_MODEL), w_in[:, off[7]:].astype(BF16),
              wa.astype(BF16), w_branch_b.astype(BF16), w_branch_m.astype(BF16), w_out.astype(BF16),
              norm2_g.reshape(1, D_MODEL), w_up.astype(BF16), w_down.astype(BF16),
              final_norm_g.reshape(1, D_MODEL), tm=512, final_norm=final_norm)
    return y.reshape(b, seq, D_MODEL)


def _trunk(x, mem, rel_bias, norm1_g, w_in, mem_norm_g, w_mem_kv, sink_logit, w_branch_a,
           w_branch_b, w_branch_m, w_out, norm2_g, w_up, w_down, final_norm_g):
    depth = w_in.shape[0]
    for layer in range(depth):
        x = _layer(x, mem, rel_bias, norm1_g[layer], w_in[layer], mem_norm_g[layer], w_mem_kv[layer],
                   sink_logit[layer], w_branch_a[layer], w_branch_b[layer], w_branch_m[layer],
                   w_out[layer], norm2_g[layer], w_up[layer], w_down[layer], final_norm_g,
                   final_norm=layer == depth - 1)
    return x


def kernel(x_prompt, x_sample, mem_prompt, mem_sample, rel_bias, norm1_g, w_in, mem_norm_g, w_mem_kv,
           sink_logit, w_branch_a, w_branch_b, w_branch_m, w_out, norm2_g, w_up, w_down, final_norm_g):
    weights = (rel_bias, norm1_g, w_in, mem_norm_g, w_mem_kv, sink_logit, w_branch_a, w_branch_b,
               w_branch_m, w_out, norm2_g, w_up, w_down, final_norm_g)
    return (_trunk(x_prompt, mem_prompt, *weights), _trunk(x_sample, mem_sample, *weights))
```

```python
import functools

import numpy as np
import jax
import jax.numpy as jnp
from jax import lax
from jax.experimental import pallas as pl
from jax.experimental.pallas import tpu as pltpu

D_MODEL = 1024
HEAD_DIM = 64
N_HEADS_A = 8
N_KV_A = 2
WIN_A = 128
DILATED_GROUPS = ((128, 1), (512, 4), (2048, 16))
HB_PER_GROUP = 4
N_HEADS_B = HB_PER_GROUP * len(DILATED_GROUPS)
N_HEADS_M = 4
HEAD_DIM_M = 128
D_FF = 4 * D_MODEL
NUM_BUCKETS = 32
MAX_DIST = 1024
EPS = 1e-6
NEG = -1e30

W_A = N_HEADS_A * HEAD_DIM
W_KV_A = N_KV_A * HEAD_DIM
W_B = N_HEADS_B * HEAD_DIM
W_B_OUT = HB_PER_GROUP * HEAD_DIM
W_M = N_HEADS_M * HEAD_DIM_M
IN_SPLITS = (W_A, W_KV_A, W_KV_A, W_B, W_B, W_B, W_M, D_MODEL, D_MODEL, D_MODEL)

LANES = 128
QBLK = 128
VMEM_LIMIT = 56 * 1024 * 1024
BF16 = jnp.bfloat16
F32 = jnp.float32


def _t5_bucket(rel):
    half = NUM_BUCKETS // 2
    ret = (rel > 0).astype(np.int32) * half
    n = np.abs(rel)
    max_exact = half // 2
    large = max_exact + (np.log(np.maximum(n, 1) / max_exact) / np.log(MAX_DIST / max_exact)
                         * (half - max_exact)).astype(np.int32)
    large = np.minimum(large, half - 1)
    return (ret + np.where(n < max_exact, n, large)).astype(np.int32)


def _bias_tiles(rel_bias, cols, half, dil, kw, deltas):
    tiles = []
    period = QBLK + kw
    k = np.arange(period)
    for delta in deltas:
        off = np.where(k < kw, k, k - period) + delta
        valid = np.abs(off) <= half
        bucket = _t5_bucket(dil * np.clip(off, -half, half))
        v = jnp.where(valid[None, :], rel_bias[bucket][:, cols].astype(F32).T, NEG)
        t = jnp.tile(v, (1, QBLK))[:, :QBLK * (period - 1)].reshape(len(cols), QBLK, period - 1)
        tiles.append(t[:, :, :kw])
    return jnp.stack(tiles)


def _params(sem):
    return pltpu.CompilerParams(dimension_semantics=sem, vmem_limit_bytes=VMEM_LIMIT)


def _rms(x, g):
    return x * lax.rsqrt(jnp.mean(x * x, axis=-1, keepdims=True) + EPS) * g


def _in_proj_kernel(x_ref, g_ref, w_ref, *refs, plan, tm):
    o_refs = refs[:len(plan)]
    h = _rms(x_ref[...], g_ref[...])
    hb = h.astype(BF16)
    if len(refs) > len(plan):
        hs_ref = refs[len(plan)]
        for j in range(D_MODEL // LANES):
            hs_ref[j] = h[:, j * LANES:(j + 1) * LANES]
    c = 0
    for o_ref, (wd, dil) in zip(o_refs, plan):
        rows = tm // dil
        if dil == 1:
            lhs = hb
        else:
            lhs = jnp.concatenate(
                [jnp.concatenate([hs_ref[j, pl.ds(r, rows, stride=dil), :]
                                  for j in range(D_MODEL // LANES)], axis=1)
                 for r in range(dil)], axis=0).astype(BF16)
        for s in range(0, wd, 512):
            e = min(wd, s + 512)
            res = jnp.dot(lhs, w_ref[:, c + s:c + e], preferred_element_type=F32).astype(o_ref.dtype)
            if dil == 1:
                o_ref[:, s:e] = res
            else:
                for r in range(dil):
                    o_ref[r, :, s:e] = res[r * rows:(r + 1) * rows]
        c += wd


def _in_proj(x, g, w, plan, tm):
    b, seq, _ = x.shape
    nt = seq // tm
    out_specs, out_shape = [], []
    for wd, dil in plan:
        if dil == 1:
            out_specs.append(pl.BlockSpec((tm, wd), lambda i, j: (i * nt + j, 0)))
            out_shape.append(jax.ShapeDtypeStruct((b * seq, wd), BF16))
        else:
            out_specs.append(pl.BlockSpec((None, dil, tm // dil, wd), lambda i, j: (i, 0, j, 0)))
            out_shape.append(jax.ShapeDtypeStruct((b, dil, seq // dil, wd), BF16))
    dilated = any(dil > 1 for _, dil in plan)
    return pl.pallas_call(
        functools.partial(_in_proj_kernel, plan=plan, tm=tm),
        grid=(b, nt),
        in_specs=[pl.BlockSpec((None, tm, D_MODEL), lambda i, j: (i, j, 0)),
                  pl.BlockSpec((1, D_MODEL), lambda i, j: (0, 0)),
                  pl.BlockSpec(w.shape, lambda i, j: (0, 0))],
        out_specs=out_specs,
        out_shape=out_shape,
        scratch_shapes=[pltpu.VMEM((D_MODEL // LANES, tm, LANES), F32)] if dilated else [],
        compiler_params=_params(("parallel", "arbitrary")),
        name="in_proj",
    )(x, g, w)


def _pair_attention(pairs):
    heads = []
    for q2, k2, v2, biases, sinks in pairs:
        lane = lax.broadcasted_iota(jnp.int32, q2.shape, 1)
        zero = jnp.zeros_like(q2)
        sinks = sinks or (None, None)
        heads.append((jnp.where(lane < HEAD_DIM, q2, zero), k2, v2, biases[0], sinks[0]))
        heads.append((jnp.where(lane >= HEAD_DIM, q2, zero), k2, v2, biases[1], sinks[1]))

    def scores(u):
        return lax.dot_general(heads[u][0], heads[u][1], (((1,), (1,)), ((), ())),
                               preferred_element_type=F32) + heads[u][3]

    outs, lses = [], []
    s_next = scores(0)
    for u, (_, _, v2, _, sink) in enumerate(heads):
        s = s_next
        if u + 1 < len(heads):
            s_next = scores(u + 1)
        m = jnp.max(s, axis=-1, keepdims=True)
        if sink is not None:
            m = jnp.maximum(m, sink)
        p = jnp.exp(s - m)
        den = jnp.sum(p, axis=-1, keepdims=True)
        if sink is not None:
            den = den + jnp.exp(sink - m)
        outs.append(jnp.dot(p.astype(BF16), v2, preferred_element_type=F32) / den)
        lses.append(m + jnp.log(den))
    lane_o = lax.broadcasted_iota(jnp.int32, outs[0].shape, 1)
    return [(jnp.where(lane_o < HEAD_DIM, outs[2 * i], outs[2 * i + 1]), lses[2 * i], lses[2 * i + 1])
            for i in range(len(pairs))]


def _window(n, nblk, half, length, kw):
    if nblk == 1:
        return 0, 0
    ws = jnp.clip(n * QBLK - half, 0, length - kw)
    var = jnp.where(n == 0, 0, jnp.where(n == nblk - 1, 2, 1))
    return pl.multiple_of(ws, HEAD_DIM), var


def _attn_win_kernel(sink_ref, q_ref, kv_ref, bias_ref, o_ref, *, seq):
    n = pl.program_id(1)
    nblk = seq // QBLK
    kw = 3 * WIN_A
    ws, var = _window(n, nblk, WIN_A, seq, kw)
    k2 = kv_ref[pl.ds(ws, kw), 0:LANES]
    v2 = kv_ref[pl.ds(ws, kw), LANES:2 * LANES]
    half_heads = N_HEADS_A // 2
    res = _pair_attention([(q_ref[:, t * LANES:(t + 1) * LANES], k2, v2,
                            (bias_ref[var, t], bias_ref[var, half_heads + t]),
                            (sink_ref[t], sink_ref[half_heads + t])) for t in range(half_heads)])
    for t, (o, _, _) in enumerate(res):
        o_ref[:, t * LANES:(t + 1) * LANES] = o.astype(o_ref.dtype)


def _attn_win(sink, qa, kva, bias, b, seq):
    nblk = seq // QBLK
    return pl.pallas_call(
        functools.partial(_attn_win_kernel, seq=seq),
        grid=(b, nblk),
        in_specs=[pl.BlockSpec(memory_space=pltpu.SMEM),
                  pl.BlockSpec((QBLK, W_A), lambda i, j: (i * nblk + j, 0)),
                  pl.BlockSpec((None, seq, 2 * LANES), lambda i, j: (i, 0, 0)),
                  pl.BlockSpec(bias.shape, lambda i, j: (0, 0, 0, 0))],
        out_specs=pl.BlockSpec((QBLK, W_A), lambda i, j: (i * nblk + j, 0)),
        out_shape=jax.ShapeDtypeStruct((b * seq, W_A), BF16),
        compiler_params=_params(("parallel", "arbitrary")),
        name="attn_win",
    )(sink, qa, kva.reshape(b, seq, 2 * LANES), bias)


def _attn_dil_kernel(qkv_ref, bias_ref, o_ref, l_ref, *, dil, length, half, kw):
    nblk = length // QBLK

    def body(it, carry):
        r = it // nblk
        n = it % nblk
        ws, var = _window(n, nblk, half, length, kw)
        qs = pl.multiple_of(n * QBLK, QBLK)
        pairs = []
        for t in range(HB_PER_GROUP // 2):
            c = t * LANES
            pairs.append((qkv_ref[r, pl.ds(qs, QBLK), c:c + LANES],
                          qkv_ref[r, pl.ds(ws, kw), W_B_OUT + c:W_B_OUT + c + LANES],
                          qkv_ref[r, pl.ds(ws, kw), 2 * W_B_OUT + c:2 * W_B_OUT + c + LANES],
                          (bias_ref[var, 2 * t], bias_ref[var, 2 * t + 1]), None))
        dst = pl.ds(qs, QBLK) if dil == 1 else pl.ds(qs * dil + r, QBLK, stride=dil)
        for t, (o, l_lo, l_hi) in enumerate(_pair_attention(pairs)):
            lane = lax.broadcasted_iota(jnp.int32, o.shape, 1)
            o_ref[t, dst, :] = o
            l_ref[t, dst, :] = jnp.where(lane < HEAD_DIM, l_lo, l_hi)
        return carry

    lax.fori_loop(0, dil * nblk, body, 0)


def _attn_dil(qkv, bias, dil, half):
    b, _, length, _ = qkv.shape
    kw = bias.shape[-1]
    npair = HB_PER_GROUP // 2
    blk = (None, npair, dil * length, LANES)
    return pl.pallas_call(
        functools.partial(_attn_dil_kernel, dil=dil, length=length, half=half, kw=kw),
        grid=(b,),
        in_specs=[pl.BlockSpec((None, dil, length, 3 * W_B_OUT), lambda i: (i, 0, 0, 0)),
                  pl.BlockSpec(bias.shape, lambda i: (0, 0, 0, 0))],
        out_specs=[pl.BlockSpec(blk, lambda i: (i, 0, 0, 0))] * 2,
        out_shape=[jax.ShapeDtypeStruct((b, npair, dil * length, LANES), F32)] * 2,
        compiler_params=_params(("parallel",)),
        name=f"attn_dil{dil}",
    )(qkv, bias)


def _attn_mem_kernel(q_ref, mkv_ref, o_ref, *, tq):
    scale = HEAD_DIM_M ** -0.5
    for h in range(N_HEADS_M):
        c = h * HEAD_DIM_M
        mk = mkv_ref[:, c:c + HEAD_DIM_M]
        mv = mkv_ref[:, W_M + c:W_M + c + HEAD_DIM_M]
        for r in range(0, tq, QBLK):
            s = lax.dot_general(q_ref[r:r + QBLK, c:c + HEAD_DIM_M], mk, (((1,), (1,)), ((), ())),
                                preferred_element_type=F32) * scale
            m = jnp.max(s, axis=-1, keepdims=True)
            p = jnp.exp(s - m)
            den = jnp.sum(p, axis=-1, keepdims=True)
            o = jnp.dot(p.astype(BF16), mv, preferred_element_type=F32) / den
            o_ref[r:r + QBLK, c:c + HEAD_DIM_M] = o.astype(o_ref.dtype)


def _attn_mem(qm, mkv, b, seq, n_mem, tq):
    nq = seq // tq
    return pl.pallas_call(
        functools.partial(_attn_mem_kernel, tq=tq),
        grid=(b, nq),
        in_specs=[pl.BlockSpec((tq, W_M), lambda i, j: (i * nq + j, 0)),
                  pl.BlockSpec((None, n_mem, 2 * W_M), lambda i, j: (i, 0, 0))],
        out_specs=pl.BlockSpec((tq, W_M), lambda i, j: (i * nq + j, 0)),
        out_shape=jax.ShapeDtypeStruct((b * seq, W_M), BF16),
        compiler_params=_params(("parallel", "arbitrary")),
        name="attn_mem",
    )(qm, mkv.reshape(b, n_mem, 2 * W_M))


def _tail_kernel(x_ref, oa_ref, om_ref, o0_ref, o1_ref, o2_ref, l0_ref, l1_ref, l2_ref,
                 g1_ref, wg_ref, wa_ref, wb_ref, wm_ref, wo_ref, g2_ref, wu_ref, wd_ref, gf_ref,
                 y_ref, *, final_norm):
    x = x_ref[...]
    h = _rms(x, g1_ref[...]).astype(BF16)

    def slabs(ref):
        return jnp.concatenate([ref[p] for p in range(ref.shape[0])], axis=1)

    l0, l1, l2 = slabs(l0_ref), slabs(l1_ref), slabs(l2_ref)
    mx = jnp.maximum(jnp.maximum(l0, l1), l2)
    e0, e1, e2 = jnp.exp(l0 - mx), jnp.exp(l1 - mx), jnp.exp(l2 - mx)
    ob = (e0 * slabs(o0_ref) + e1 * slabs(o1_ref) + e2 * slabs(o2_ref)) / (e0 + e1 + e2)

    def gate(k):
        return jax.nn.sigmoid(jnp.dot(h, wg_ref[:, k * D_MODEL:(k + 1) * D_MODEL],
                                      preferred_element_type=F32))

    merged = gate(0) * jnp.dot(oa_ref[...], wa_ref[...], preferred_element_type=F32)
    merged += gate(1) * jnp.dot(ob.astype(BF16), wb_ref[...], preferred_element_type=F32)
    merged += gate(2) * jnp.dot(om_ref[...], wm_ref[...], preferred_element_type=F32)
    x = x + jnp.dot(merged.astype(BF16), wo_ref[...], preferred_element_type=F32)

    h2 = _rms(x, g2_ref[...]).astype(BF16)
    ck = 1024
    for c in range(0, D_FF, ck):
        u = jnp.dot(h2, wu_ref[:, c:c + ck], preferred_element_type=F32)
        a = jnp.square(jnp.maximum(u, 0.0)).astype(BF16)
        x = x + jnp.dot(a, wd_ref[c:c + ck, :], preferred_element_type=F32)
    if final_norm:
        x = _rms(x, gf_ref[...])
    y_ref[...] = x


def _tail(x2d, oa, om, obs, lbs, g1, wg, wa, wb, wm, wo, g2, wu, wd, gf, tm, final_norm):
    t = x2d.shape[0]
    nt = obs[0].shape[2] // tm

    def rows(width):
        return pl.BlockSpec((tm, width), lambda i: (i, 0))

    def whole(a):
        return pl.BlockSpec(a.shape, lambda i: (0, 0), pipeline_mode=pl.Buffered(1))

    slab = pl.BlockSpec((None, HB_PER_GROUP // 2, tm, LANES), lambda i: (i // nt, 0, i % nt, 0))
    consts = (g1, wg, wa, wb, wm, wo, g2, wu, wd, gf)
    return pl.pallas_call(
        functools.partial(_tail_kernel, final_norm=final_norm),
        grid=(t // tm,),
        in_specs=[rows(D_MODEL), rows(W_A), rows(W_M)] + [slab] * 6 + [whole(a) for a in consts],
        out_specs=rows(D_MODEL),
        out_shape=jax.ShapeDtypeStruct((t, D_MODEL), F32),
        compiler_params=_params(("parallel",)),
        name="tail",
    )(x2d, oa, om, *obs, *lbs, *consts)


def _layer(x, mem, rel_bias, norm1_g, w_in, mem_norm_g, w_mem_kv, sink_logit, w_branch_a,
           w_branch_b, w_branch_m, w_out, norm2_g, w_up, w_down, final_norm_g, final_norm):
    b, seq, _ = x.shape
    n_mem = mem.shape[1]
    t = b * seq
    x2d = x.reshape(t, D_MODEL)

    off = np.cumsum((0,) + IN_SPLITS)
    half_heads = N_HEADS_A // 2
    scale = HEAD_DIM ** -0.5
    w_qa = (w_in[:, off[0]:off[1]].reshape(D_MODEL, 2, half_heads, HEAD_DIM)
            .transpose(0, 2, 1, 3).reshape(D_MODEL, W_A)) * scale
    parts = [w_qa, w_in[:, off[1]:off[3]]]
    for g in range(len(DILATED_GROUPS)):
        lo, hi = g * W_B_OUT, (g + 1) * W_B_OUT
        parts += [w_in[:, off[3] + lo:off[3] + hi] * scale, w_in[:, off[4] + lo:off[4] + hi],
                  w_in[:, off[5] + lo:off[5] + hi]]
    parts.append(w_in[:, off[6]:off[7]])
    w_qkv = jnp.concatenate(parts, axis=1).astype(BF16)
    plan = ((W_A, 1), (2 * W_KV_A, 1)) + tuple((3 * W_B_OUT, dil) for _, dil in DILATED_GROUPS) + ((W_M, 1),)
    qa, kva, qkv0, qkv1, qkv2, qm = _in_proj(x, norm1_g.reshape(1, D_MODEL), w_qkv, plan, tm=512)

    bias_a = _bias_tiles(rel_bias, np.arange(N_HEADS_A), WIN_A, 1, 3 * WIN_A, (0, -WIN_A, -2 * WIN_A))
    oa = _attn_win(sink_logit.astype(F32), qa, kva, bias_a, b, seq)

    obs, lbs = [], []
    for g, ((win, dil), qkv) in enumerate(zip(DILATED_GROUPS, (qkv0, qkv1, qkv2))):
        half = win // (2 * dil)
        length = seq // dil
        kw = min(4 * half, length)
        deltas = (0, -half, -2 * half) if length > kw else (0,)
        hcols = N_HEADS_A + g * HB_PER_GROUP + np.arange(HB_PER_GROUP)
        bias_g = _bias_tiles(rel_bias, hcols, half, dil, kw, deltas)
        o, l = _attn_dil(qkv.reshape(b, dil, length, 3 * W_B_OUT), bias_g, dil, half)
        obs.append(o)
        lbs.append(l)

    mkv, = _in_proj(mem.reshape(1, b * n_mem, D_MODEL), mem_norm_g.reshape(1, D_MODEL),
                    w_mem_kv.astype(BF16), ((2 * W_M, 1),), tm=512)
    om = _attn_mem(qm, mkv, b, seq, n_mem, tq=512)

    wa = (w_branch_a.reshape(2, half_heads, HEAD_DIM, D_MODEL).transpose(1, 0, 2, 3)
          .reshape(W_A, D_MODEL))
    y = _tail(x2d, oa, om, obs, lbs, norm1_g.reshape(1, D_MODEL), w_in[:, off[7]:].astype(BF16),
              wa.astype(BF16), w_branch_b.astype(BF16), w_branch_m.astype(BF16), w_out.astype(BF16),
              norm2_g.reshape(1, D_MODEL), w_up.astype(BF16), w_down.astype(BF16),
              final_norm_g.reshape(1, D_MODEL), tm=512, final_norm=final_norm)
    return y.reshape(b, seq, D_MODEL)


def _trunk(x, mem, rel_bias, norm1_g, w_in, mem_norm_g, w_mem_kv, sink_logit, w_branch_a,
           w_branch_b, w_branch_m, w_out, norm2_g, w_up, w_down, final_norm_g):
    depth = w_in.shape[0]
    for layer in range(depth):
        x = _layer(x, mem, rel_bias, norm1_g[layer], w_in[layer], mem_norm_g[layer], w_mem_kv[layer],
                   sink_logit[layer], w_branch_a[layer], w_branch_b[layer], w_branch_m[layer],
                   w_out[layer], norm2_g[layer], w_up[layer], w_down[layer], final_norm_g,
                   final_norm=layer == depth - 1)
    return x


def kernel(x_prompt, x_sample, mem_prompt, mem_sample, rel_bias, norm1_g, w_in, mem_norm_g, w_mem_kv,
           sink_logit, w_branch_a, w_branch_b, w_branch_m, w_out, norm2_g, w_up, w_down, final_norm_g):
    weights = (rel_bias, norm1_g, w_in, mem_norm_g, w_mem_kv, sink_logit, w_branch_a, w_branch_b,
               w_branch_m, w_out, norm2_g, w_up, w_down, final_norm_g)
    return (_trunk(x_prompt, mem_prompt, *weights), _trunk(x_sample, mem_sample, *weights))
```

```python
import functools

import numpy as np
import jax
import jax.numpy as jnp
from jax import lax
from jax.experimental import pallas as pl
from jax.experimental.pallas import tpu as pltpu

D_MODEL = 1024
HEAD_DIM = 64
N_HEADS_A = 8
N_KV_A = 2
WIN_A = 128
DILATED_GROUPS = ((128, 1), (512, 4), (2048, 16))
HB_PER_GROUP = 4
N_HEADS_B = HB_PER_GROUP * len(DILATED_GROUPS)
N_HEADS_M = 4
HEAD_DIM_M = 128
D_FF = 4 * D_MODEL
NUM_BUCKETS = 32
MAX_DIST = 1024
EPS = 1e-6
NEG = -1e30

W_A = N_HEADS_A * HEAD_DIM
W_KV_A = N_KV_A * HEAD_DIM
W_B = N_HEADS_B * HEAD_DIM
W_B_OUT = HB_PER_GROUP * HEAD_DIM
W_M = N_HEADS_M * HEAD_DIM_M
IN_SPLITS = (W_A, W_KV_A, W_KV_A, W_B, W_B, W_B, W_M, D_MODEL, D_MODEL, D_MODEL)

LANES = 128
QBLK = 128
LOOKAHEAD = 2
VMEM_LIMIT = 56 * 1024 * 1024
BF16 = jnp.bfloat16
F32 = jnp.float32


def _t5_bucket(rel):
    half = NUM_BUCKETS // 2
    ret = (rel > 0).astype(np.int32) * half
    n = np.abs(rel)
    max_exact = half // 2
    large = max_exact + (np.log(np.maximum(n, 1) / max_exact) / np.log(MAX_DIST / max_exact)
                         * (half - max_exact)).astype(np.int32)
    large = np.minimum(large, half - 1)
    return (ret + np.where(n < max_exact, n, large)).astype(np.int32)


def _bias_tiles(rel_bias, cols, half, dil, kw, deltas):
    tiles = []
    period = QBLK + kw
    k = np.arange(period)
    for delta in deltas:
        off = np.where(k < kw, k, k - period) + delta
        valid = np.abs(off) <= half
        bucket = _t5_bucket(dil * np.clip(off, -half, half))
        v = jnp.where(valid[None, :], rel_bias[bucket][:, cols].astype(F32).T, NEG)
        t = jnp.tile(v, (1, QBLK))[:, :QBLK * (period - 1)].reshape(len(cols), QBLK, period - 1)
        tiles.append(t[:, :, :kw])
    return jnp.stack(tiles)


def _params(sem):
    return pltpu.CompilerParams(dimension_semantics=sem, vmem_limit_bytes=VMEM_LIMIT)


def _rms(x, g):
    return x * lax.rsqrt(jnp.mean(x * x, axis=-1, keepdims=True) + EPS) * g


def _in_proj_kernel(x_ref, g_ref, w_ref, *refs, plan, tm):
    o_refs = refs[:len(plan)]
    h = _rms(x_ref[...], g_ref[...])
    hb = h.astype(BF16)
    if len(refs) > len(plan):
        hs_ref = refs[len(plan)]
        for j in range(D_MODEL // LANES):
            hs_ref[j] = h[:, j * LANES:(j + 1) * LANES]
    c = 0
    for o_ref, (wd, dil) in zip(o_refs, plan):
        rows = tm // dil
        if dil == 1:
            lhs = hb
        else:
            lhs = jnp.concatenate(
                [jnp.concatenate([hs_ref[j, pl.ds(r, rows, stride=dil), :]
                                  for j in range(D_MODEL // LANES)], axis=1)
                 for r in range(dil)], axis=0).astype(BF16)
        for s in range(0, wd, 512):
            e = min(wd, s + 512)
            res = jnp.dot(lhs, w_ref[:, c + s:c + e], preferred_element_type=F32).astype(o_ref.dtype)
            if dil == 1:
                o_ref[:, s:e] = res
            else:
                for r in range(dil):
                    o_ref[r, :, s:e] = res[r * rows:(r + 1) * rows]
        c += wd


def _in_proj(x, g, w, plan, tm):
    b, seq, _ = x.shape
    nt = seq // tm
    out_specs, out_shape = [], []
    for wd, dil in plan:
        if dil == 1:
            out_specs.append(pl.BlockSpec((tm, wd), lambda i, j: (i * nt + j, 0)))
            out_shape.append(jax.ShapeDtypeStruct((b * seq, wd), BF16))
        else:
            out_specs.append(pl.BlockSpec((None, dil, tm // dil, wd), lambda i, j: (i, 0, j, 0)))
            out_shape.append(jax.ShapeDtypeStruct((b, dil, seq // dil, wd), BF16))
    dilated = any(dil > 1 for _, dil in plan)
    return pl.pallas_call(
        functools.partial(_in_proj_kernel, plan=plan, tm=tm),
        grid=(b, nt),
        in_specs=[pl.BlockSpec((None, tm, D_MODEL), lambda i, j: (i, j, 0)),
                  pl.BlockSpec((1, D_MODEL), lambda i, j: (0, 0)),
                  pl.BlockSpec(w.shape, lambda i, j: (0, 0))],
        out_specs=out_specs,
        out_shape=out_shape,
        scratch_shapes=[pltpu.VMEM((D_MODEL // LANES, tm, LANES), F32)] if dilated else [],
        compiler_params=_params(("parallel", "arbitrary")),
        name="in_proj",
    )(x, g, w)


def _attention_stream(heads, emit, lookahead):
    cache = {}

    def scores(u):
        cache[u] = heads[u]()
        q, k2, _, bias, _ = cache[u]
        return lax.dot_general(q, k2, (((1,), (1,)), ((), ())), preferred_element_type=F32) + bias

    pending = [scores(u) for u in range(min(lookahead, len(heads)))]
    for u in range(len(heads)):
        s = pending.pop(0)
        if u + lookahead < len(heads):
            pending.append(scores(u + lookahead))
        _, _, v2, _, sink = cache.pop(u)
        m = jnp.max(s, axis=-1, keepdims=True)
        if sink is not None:
            m = jnp.maximum(m, sink)
        p = jnp.exp(s - m)
        den = jnp.sum(p, axis=-1, keepdims=True)
        if sink is not None:
            den = den + jnp.exp(sink - m)
        emit(u, jnp.dot(p.astype(BF16), v2, preferred_element_type=F32) / den, m + jnp.log(den))


def _half(q2, hi):
    lane = lax.broadcasted_iota(jnp.int32, q2.shape, 1)
    keep = lane >= HEAD_DIM if hi else lane < HEAD_DIM
    return jnp.where(keep, q2, jnp.zeros_like(q2))


def _window(n, nblk, half, length, kw):
    if nblk == 1:
        return 0, 0
    return min(max(n * QBLK - half, 0), length - kw), (0 if n == 0 else 2 if n == nblk - 1 else 1)


def _attn_win_kernel(sink_ref, q_ref, kv_ref, bias_ref, o_ref, *, seq):
    nblk = seq // QBLK
    kw = 3 * WIN_A
    half_heads = N_HEADS_A // 2
    heads, where = [], []
    for n in range(nblk):
        ws, var = _window(n, nblk, WIN_A, seq, kw)
        for t in range(half_heads):
            for hi in (False, True):
                h = half_heads * hi + t

                def head(n=n, t=t, hi=hi, h=h, ws=ws, var=var):
                    return (_half(q_ref[n * QBLK:(n + 1) * QBLK, t * LANES:(t + 1) * LANES], hi),
                            kv_ref[ws:ws + kw, 0:LANES], kv_ref[ws:ws + kw, LANES:2 * LANES],
                            bias_ref[var, h], sink_ref[h])

                heads.append(head)
                where.append((n, t))
    held = {}

    def emit(u, o, _):
        if u % 2 == 0:
            held[u] = o
            return
        n, t = where[u]
        lane = lax.broadcasted_iota(jnp.int32, o.shape, 1)
        tile = jnp.where(lane < HEAD_DIM, held.pop(u - 1), o)
        o_ref[n * QBLK:(n + 1) * QBLK, t * LANES:(t + 1) * LANES] = tile.astype(o_ref.dtype)

    _attention_stream(heads, emit, LOOKAHEAD)


def _attn_win(sink, qa, kva, bias, b, seq):
    return pl.pallas_call(
        functools.partial(_attn_win_kernel, seq=seq),
        grid=(b,),
        in_specs=[pl.BlockSpec(memory_space=pltpu.SMEM),
                  pl.BlockSpec((seq, W_A), lambda i: (i, 0)),
                  pl.BlockSpec((seq, 2 * LANES), lambda i: (i, 0)),
                  pl.BlockSpec(bias.shape, lambda i: (0, 0, 0, 0))],
        out_specs=pl.BlockSpec((seq, W_A), lambda i: (i, 0)),
        out_shape=jax.ShapeDtypeStruct((b * seq, W_A), BF16),
        compiler_params=_params(("parallel",)),
        name="attn_win",
    )(sink, qa, kva, bias)


def _attn_dil_kernel(qkv_ref, bias_ref, o_ref, l_ref, *, dil, length, half, kw):
    nblk = length // QBLK
    heads, where = [], []
    for r in range(dil):
        for n in range(nblk):
            ws, var = _window(n, nblk, half, length, kw)
            for t in range(HB_PER_GROUP // 2):
                for hi in (False, True):
                    c = t * LANES

                    def head(r=r, n=n, c=c, hi=hi, ws=ws, var=var, h=2 * t + hi):
                        return (_half(qkv_ref[r, n * QBLK:(n + 1) * QBLK, c:c + LANES], hi),
                                qkv_ref[r, ws:ws + kw, W_B_OUT + c:W_B_OUT + c + LANES],
                                qkv_ref[r, ws:ws + kw, 2 * W_B_OUT + c:2 * W_B_OUT + c + LANES],
                                bias_ref[var, h], None)

                    heads.append(head)
                    where.append((r, n, t))
    held = {}

    def emit(u, o, lse):
        if u % 2 == 0:
            held[u] = (o, lse)
            return
        r, n, t = where[u]
        o_lo, l_lo = held.pop(u - 1)
        lane = lax.broadcasted_iota(jnp.int32, o.shape, 1)
        dst = pl.ds(n * QBLK * dil + r, QBLK, stride=dil) if dil > 1 else pl.ds(n * QBLK, QBLK)
        o_ref[t, dst, :] = jnp.where(lane < HEAD_DIM, o_lo, o)
        l_ref[t, dst, :] = jnp.where(lane < HEAD_DIM, l_lo, lse)

    _attention_stream(heads, emit, LOOKAHEAD)


def _attn_dil(qkv, bias, dil, half):
    b, _, length, _ = qkv.shape
    kw = bias.shape[-1]
    npair = HB_PER_GROUP // 2
    blk = (None, npair, dil * length, LANES)
    return pl.pallas_call(
        functools.partial(_attn_dil_kernel, dil=dil, length=length, half=half, kw=kw),
        grid=(b,),
        in_specs=[pl.BlockSpec((None, dil, length, 3 * W_B_OUT), lambda i: (i, 0, 0, 0)),
                  pl.BlockSpec(bias.shape, lambda i: (0, 0, 0, 0))],
        out_specs=[pl.BlockSpec(blk, lambda i: (i, 0, 0, 0))] * 2,
        out_shape=[jax.ShapeDtypeStruct((b, npair, dil * length, LANES), F32)] * 2,
        compiler_params=_params(("parallel",)),
        name=f"attn_dil{dil}",
    )(qkv, bias)


def _attn_mem_kernel(q_ref, mkv_ref, o_ref, *, tq):
    scale = HEAD_DIM_M ** -0.5
    for h in range(N_HEADS_M):
        c = h * HEAD_DIM_M
        mk = mkv_ref[:, c:c + HEAD_DIM_M]
        mv = mkv_ref[:, W_M + c:W_M + c + HEAD_DIM_M]
        for r in range(0, tq, QBLK):
            s = lax.dot_general(q_ref[r:r + QBLK, c:c + HEAD_DIM_M], mk, (((1,), (1,)), ((), ())),
                                preferred_element_type=F32) * scale
            m = jnp.max(s, axis=-1, keepdims=True)
            p = jnp.exp(s - m)
            den = jnp.sum(p, axis=-1, keepdims=True)
            o = jnp.dot(p.astype(BF16), mv, preferred_element_type=F32) / den
            o_ref[r:r + QBLK, c:c + HEAD_DIM_M] = o.astype(o_ref.dtype)


def _attn_mem(qm, mkv, b, seq, n_mem, tq):
    nq = seq // tq
    return pl.pallas_call(
        functools.partial(_attn_mem_kernel, tq=tq),
        grid=(b, nq),
        in_specs=[pl.BlockSpec((tq, W_M), lambda i, j: (i * nq + j, 0)),
                  pl.BlockSpec((None, n_mem, 2 * W_M), lambda i, j: (i, 0, 0))],
        out_specs=pl.BlockSpec((tq, W_M), lambda i, j: (i * nq + j, 0)),
        out_shape=jax.ShapeDtypeStruct((b * seq, W_M), BF16),
        compiler_params=_params(("parallel", "arbitrary")),
        name="attn_mem",
    )(qm, mkv.reshape(b, n_mem, 2 * W_M))


def _tail_kernel(x_ref, oa_ref, om_ref, o0_ref, o1_ref, o2_ref, l0_ref, l1_ref, l2_ref,
                 g1_ref, wg_ref, wa_ref, wb_ref, wm_ref, wo_ref, g2_ref, wu_ref, wd_ref, gf_ref,
                 y_ref, *, final_norm):
    x = x_ref[...]
    h = _rms(x, g1_ref[...]).astype(BF16)

    def slabs(ref):
        return jnp.concatenate([ref[p] for p in range(ref.shape[0])], axis=1)

    l0, l1, l2 = slabs(l0_ref), slabs(l1_ref), slabs(l2_ref)
    mx = jnp.maximum(jnp.maximum(l0, l1), l2)
    e0, e1, e2 = jnp.exp(l0 - mx), jnp.exp(l1 - mx), jnp.exp(l2 - mx)
    ob = (e0 * slabs(o0_ref) + e1 * slabs(o1_ref) + e2 * slabs(o2_ref)) / (e0 + e1 + e2)

    def gate(k):
        return jax.nn.sigmoid(jnp.dot(h, wg_ref[:, k * D_MODEL:(k + 1) * D_MODEL],
                                      preferred_element_type=F32))

    merged = gate(0) * jnp.dot(oa_ref[...], wa_ref[...], preferred_element_type=F32)
    merged += gate(1) * jnp.dot(ob.astype(BF16), wb_ref[...], preferred_element_type=F32)
    merged += gate(2) * jnp.dot(om_ref[...], wm_ref[...], preferred_element_type=F32)
    x = x + jnp.dot(merged.astype(BF16), wo_ref[...], preferred_element_type=F32)

    h2 = _rms(x, g2_ref[...]).astype(BF16)
    ck = 1024
    for c in range(0, D_FF, ck):
        u = jnp.dot(h2, wu_ref[:, c:c + ck], preferred_element_type=F32)
        a = jnp.square(jnp.maximum(u, 0.0)).astype(BF16)
        x = x + jnp.dot(a, wd_ref[c:c + ck, :], preferred_element_type=F32)
    if final_norm:
        x = _rms(x, gf_ref[...])
    y_ref[...] = x


def _tail(x2d, oa, om, obs, lbs, g1, wg, wa, wb, wm, wo, g2, wu, wd, gf, tm, final_norm):
    t = x2d.shape[0]
    nt = obs[0].shape[2] // tm

    def rows(width):
        return pl.BlockSpec((tm, width), lambda i: (i, 0))

    def whole(a):
        return pl.BlockSpec(a.shape, lambda i: (0, 0), pipeline_mode=pl.Buffered(1))

    slab = pl.BlockSpec((None, HB_PER_GROUP // 2, tm, LANES), lambda i: (i // nt, 0, i % nt, 0))
    consts = (g1, wg, wa, wb, wm, wo, g2, wu, wd, gf)
    return pl.pallas_call(
        functools.partial(_tail_kernel, final_norm=final_norm),
        grid=(t // tm,),
        in_specs=[rows(D_MODEL), rows(W_A), rows(W_M)] + [slab] * 6 + [whole(a) for a in consts],
        out_specs=rows(D_MODEL),
        out_shape=jax.ShapeDtypeStruct((t, D_MODEL), F32),
        compiler_params=_params(("parallel",)),
        name="tail",
    )(x2d, oa, om, *obs, *lbs, *consts)


def _layer(x, mem, rel_bias, norm1_g, w_in, mem_norm_g, w_mem_kv, sink_logit, w_branch_a,
           w_branch_b, w_branch_m, w_out, norm2_g, w_up, w_down, final_norm_g, final_norm):
    b, seq, _ = x.shape
    n_mem = mem.shape[1]
    t = b * seq
    x2d = x.reshape(t, D_MODEL)

    off = np.cumsum((0,) + IN_SPLITS)
    half_heads = N_HEADS_A // 2
    scale = HEAD_DIM ** -0.5
    w_qa = (w_in[:, off[0]:off[1]].reshape(D_MODEL, 2, half_heads, HEAD_DIM)
            .transpose(0, 2, 1, 3).reshape(D_MODEL, W_A)) * scale
    parts = [w_qa, w_in[:, off[1]:off[3]]]
    for g in range(len(DILATED_GROUPS)):
        lo, hi = g * W_B_OUT, (g + 1) * W_B_OUT
        parts += [w_in[:, off[3] + lo:off[3] + hi] * scale, w_in[:, off[4] + lo:off[4] + hi],
                  w_in[:, off[5] + lo:off[5] + hi]]
    parts.append(w_in[:, off[6]:off[7]])
    w_qkv = jnp.concatenate(parts, axis=1).astype(BF16)
    plan = ((W_A, 1), (2 * W_KV_A, 1)) + tuple((3 * W_B_OUT, dil) for _, dil in DILATED_GROUPS) + ((W_M, 1),)
    qa, kva, qkv0, qkv1, qkv2, qm = _in_proj(x, norm1_g.reshape(1, D_MODEL), w_qkv, plan, tm=512)

    bias_a = _bias_tiles(rel_bias, np.arange(N_HEADS_A), WIN_A, 1, 3 * WIN_A, (0, -WIN_A, -2 * WIN_A))
    oa = _attn_win(sink_logit.astype(F32), qa, kva, bias_a, b, seq)

    obs, lbs = [], []
    for g, ((win, dil), qkv) in enumerate(zip(DILATED_GROUPS, (qkv0, qkv1, qkv2))):
        half = win // (2 * dil)
        length = seq // dil
        kw = min(4 * half, length)
        deltas = (0, -half, -2 * half) if length > kw else (0,)
        hcols = N_HEADS_A + g * HB_PER_GROUP + np.arange(HB_PER_GROUP)
        bias_g = _bias_tiles(rel_bias, hcols, half, dil, kw, deltas)
        o, l = _attn_dil(qkv.reshape(b, dil, length, 3 * W_B_OUT), bias_g, dil, half)
        obs.append(o)
        lbs.append(l)

    mkv, = _in_proj(mem.reshape(1, b * n_mem, D_MODEL), mem_norm_g.reshape(1, D_MODEL),
                    w_mem_kv.astype(BF16), ((2 * W_M, 1),), tm=512)
    om = _attn_mem(qm, mkv, b, seq, n_mem, tq=512)

    wa = (w_branch_a.reshape(2, half_heads, HEAD_DIM, D_MODEL).transpose(1, 0, 2, 3)
          .reshape(W_A, D_MODEL))
    y = _tail(x2d, oa, om, obs, lbs, norm1_g.reshape(1, D_MODEL), w_in[:, off[7]:].astype(BF16),
              wa.astype(BF16), w_branch_b.astype(BF16), w_branch_m.astype(BF16), w_out.astype(BF16),
              norm2_g.reshape(1, D_MODEL), w_up.astype(BF16), w_down.astype(BF16),
              final_norm_g.reshape(1, D_MODEL), tm=512, final_norm=final_norm)
    return y.reshape(b, seq, D_MODEL)


def _trunk(x, mem, rel_bias, norm1_g, w_in, mem_norm_g, w_mem_kv, sink_logit, w_branch_a,
           w_branch_b, w_branch_m, w_out, norm2_g, w_up, w_down, final_norm_g):
    depth = w_in.shape[0]
    for layer in range(depth):
        x = _layer(x, mem, rel_bias, norm1_g[layer], w_in[layer], mem_norm_g[layer], w_mem_kv[layer],
                   sink_logit[layer], w_branch_a[layer], w_branch_b[layer], w_branch_m[layer],
                   w_out[layer], norm2_g[layer], w_up[layer], w_down[layer], final_norm_g,
                   final_norm=layer == depth - 1)
    return x


def kernel(x_prompt, x_sample, mem_prompt, mem_sample, rel_bias, norm1_g, w_in, mem_norm_g, w_mem_kv,
           sink_logit, w_branch_a, w_branch_b, w_branch_m, w_out, norm2_g, w_up, w_down, final_norm_g):
    weights = (rel_bias, norm1_g, w_in, mem_norm_g, w_mem_kv, sink_logit, w_branch_a, w_branch_b,
               w_branch_m, w_out, norm2_g, w_up, w_down, final_norm_g)
    return (_trunk(x_prompt, mem_prompt, *weights), _trunk(x_sample, mem_sample, *weights))
```

```python
import functools

import numpy as np
import jax
import jax.numpy as jnp
from jax import lax
from jax.experimental import pallas as pl
from jax.experimental.pallas import tpu as pltpu

D_MODEL = 1024
HEAD_DIM = 64
N_HEADS_A = 8
N_KV_A = 2
WIN_A = 128
DILATED_GROUPS = ((128, 1), (512, 4), (2048, 16))
HB_PER_GROUP = 4
N_HEADS_B = HB_PER_GROUP * len(DILATED_GROUPS)
N_HEADS_M = 4
HEAD_DIM_M = 128
D_FF = 4 * D_MODEL
NUM_BUCKETS = 32
MAX_DIST = 1024
EPS = 1e-6
NEG = -1e30

W_A = N_HEADS_A * HEAD_DIM
W_KV_A = N_KV_A * HEAD_DIM
W_B = N_HEADS_B * HEAD_DIM
W_B_OUT = HB_PER_GROUP * HEAD_DIM
W_M = N_HEADS_M * HEAD_DIM_M
IN_SPLITS = (W_A, W_KV_A, W_KV_A, W_B, W_B, W_B, W_M, D_MODEL, D_MODEL, D_MODEL)

LANES = 128
QBLK = 128
LOOKAHEAD = 2
LOOKAHEAD_DIL = 3
LOG2E = 1.4426950408889634
VMEM_LIMIT = 56 * 1024 * 1024
BF16 = jnp.bfloat16
F32 = jnp.float32


def _t5_bucket(rel):
    half = NUM_BUCKETS // 2
    ret = (rel > 0).astype(np.int32) * half
    n = np.abs(rel)
    max_exact = half // 2
    large = max_exact + (np.log(np.maximum(n, 1) / max_exact) / np.log(MAX_DIST / max_exact)
                         * (half - max_exact)).astype(np.int32)
    large = np.minimum(large, half - 1)
    return (ret + np.where(n < max_exact, n, large)).astype(np.int32)


def _bias_tiles(rel_bias, cols, half, dil, kw, deltas):
    tiles = []
    period = QBLK + kw
    k = np.arange(period)
    for delta in deltas:
        off = np.where(k < kw, k, k - period) + delta
        valid = np.abs(off) <= half
        bucket = _t5_bucket(dil * np.clip(off, -half, half))
        v = jnp.where(valid[None, :], rel_bias[bucket][:, cols].astype(F32).T * LOG2E, NEG)
        t = jnp.tile(v, (1, QBLK))[:, :QBLK * (period - 1)].reshape(len(cols), QBLK, period - 1)
        tiles.append(t[:, :, :kw])
    return jnp.stack(tiles)


def _params(sem):
    return pltpu.CompilerParams(dimension_semantics=sem, vmem_limit_bytes=VMEM_LIMIT)


def _rms(x, g):
    return x * lax.rsqrt(jnp.mean(x * x, axis=-1, keepdims=True) + EPS) * g


def _in_proj_kernel(x_ref, g_ref, w_ref, *refs, plan, tm):
    o_refs = refs[:len(plan)]
    h = _rms(x_ref[...], g_ref[...])
    hb = h.astype(BF16)
    if len(refs) > len(plan):
        hs_ref = refs[len(plan)]
        for j in range(D_MODEL // LANES):
            hs_ref[j] = h[:, j * LANES:(j + 1) * LANES]
    c = 0
    for o_ref, (wd, dil) in zip(o_refs, plan):
        rows = tm // dil
        if dil == 1:
            lhs = hb
        else:
            lhs = jnp.concatenate(
                [jnp.concatenate([hs_ref[j, pl.ds(r, rows, stride=dil), :]
                                  for j in range(D_MODEL // LANES)], axis=1)
                 for r in range(dil)], axis=0).astype(BF16)
        for s in range(0, wd, 512):
            e = min(wd, s + 512)
            res = jnp.dot(lhs, w_ref[:, c + s:c + e], preferred_element_type=F32).astype(o_ref.dtype)
            if dil == 1:
                o_ref[:, s:e] = res
            else:
                for r in range(dil):
                    o_ref[r, :, s:e] = res[r * rows:(r + 1) * rows]
        c += wd


def _in_proj(x, g, w, plan, tm):
    b, seq, _ = x.shape
    nt = seq // tm
    out_specs, out_shape = [], []
    for wd, dil in plan:
        if dil == 1:
            out_specs.append(pl.BlockSpec((tm, wd), lambda i, j: (i * nt + j, 0)))
            out_shape.append(jax.ShapeDtypeStruct((b * seq, wd), BF16))
        else:
            out_specs.append(pl.BlockSpec((None, dil, tm // dil, wd), lambda i, j: (i, 0, j, 0)))
            out_shape.append(jax.ShapeDtypeStruct((b, dil, seq // dil, wd), BF16))
    dilated = any(dil > 1 for _, dil in plan)
    return pl.pallas_call(
        functools.partial(_in_proj_kernel, plan=plan, tm=tm),
        grid=(b, nt),
        in_specs=[pl.BlockSpec((None, tm, D_MODEL), lambda i, j: (i, j, 0)),
                  pl.BlockSpec((1, D_MODEL), lambda i, j: (0, 0)),
                  pl.BlockSpec(w.shape, lambda i, j: (0, 0))],
        out_specs=out_specs,
        out_shape=out_shape,
        scratch_shapes=[pltpu.VMEM((D_MODEL // LANES, tm, LANES), F32)] if dilated else [],
        compiler_params=_params(("parallel", "arbitrary")),
        name="in_proj",
    )(x, g, w)


def _pair_stream(units, emit, lookahead):
    cache = {}

    def scores(u):
        cache[u] = units[u]()
        return cache[u][0]()

    def parts(s, sink):
        m = jnp.max(s, axis=-1, keepdims=True)
        if sink is not None:
            m = jnp.maximum(m, sink)
        p = jnp.exp2(s - m)
        return m, p.astype(BF16), jnp.sum(p, axis=-1, keepdims=True)

    pending = [scores(u) for u in range(min(lookahead, len(units)))]
    for u in range(len(units)):
        s_lo, s_hi = pending.pop(0)
        if u + lookahead < len(units):
            pending.append(scores(u + lookahead))
        _, value_fn, sinks = cache.pop(u)
        m_lo, p_lo, d_lo = parts(s_lo, sinks and sinks[0])
        m_hi, p_hi, d_hi = parts(s_hi, sinks and sinks[1])
        o = value_fn(p_lo, p_hi)
        low = lax.broadcasted_iota(jnp.int32, o.shape, 1) < HEAD_DIM
        m = jnp.where(low, m_lo, m_hi)
        den = jnp.where(low, d_lo, d_hi)
        if sinks:
            den = den + jnp.exp2(jnp.where(low, sinks[0], sinks[1]) - m)
        emit(u, o / den, m + jnp.log2(den))


def _half(x2, hi):
    lane = lax.broadcasted_iota(jnp.int32, x2.shape, 1)
    keep = lane >= HEAD_DIM if hi else lane < HEAD_DIM
    return jnp.where(keep, x2, jnp.zeros_like(x2))


def _window(n, nblk, half, length, kw):
    if nblk == 1:
        return 0, 0
    return min(max(n * QBLK - half, 0), length - kw), (0 if n == 0 else 2 if n == nblk - 1 else 1)


_NT = (((1,), (1,)), ((), ()))


def _attn_win_kernel(sink_ref, q_ref, kv_ref, bias_ref, o_ref, *, seq):
    nblk = seq // QBLK
    kw = 3 * WIN_A
    half_heads = N_HEADS_A // 2
    stacks = {}

    def stacked(n, ws):
        if n not in stacks:
            k2 = kv_ref[ws:ws + kw, 0:LANES]
            v2 = kv_ref[ws:ws + kw, LANES:2 * LANES]
            stacks.clear()
            stacks[n] = (jnp.concatenate([_half(k2, False), _half(k2, True)], axis=0),
                         jnp.concatenate([_half(v2, False), _half(v2, True)], axis=0))
        return stacks[n]

    units, where = [], []
    for n in range(nblk):
        ws, var = _window(n, nblk, WIN_A, seq, kw)
        for t in range(half_heads):

            def unit(n=n, t=t, ws=ws, var=var):
                km, vm = stacked(n, ws)

                def score_fn():
                    s = lax.dot_general(q_ref[n * QBLK:(n + 1) * QBLK, t * LANES:(t + 1) * LANES], km,
                                        _NT, preferred_element_type=F32)
                    return s[:, :kw] + bias_ref[var, t], s[:, kw:] + bias_ref[var, half_heads + t]

                def value_fn(p_lo, p_hi):
                    return jnp.dot(jnp.concatenate([p_lo, p_hi], axis=1), vm, preferred_element_type=F32)

                return score_fn, value_fn, (sink_ref[t], sink_ref[half_heads + t])

            units.append(unit)
            where.append((n, t))

    def emit(u, o, _):
        n, t = where[u]
        o_ref[n * QBLK:(n + 1) * QBLK, t * LANES:(t + 1) * LANES] = o.astype(o_ref.dtype)

    _pair_stream(units, emit, LOOKAHEAD)


def _attn_win(sink, qa, kva, bias, b, seq):
    return pl.pallas_call(
        functools.partial(_attn_win_kernel, seq=seq),
        grid=(b,),
        in_specs=[pl.BlockSpec(memory_space=pltpu.SMEM),
                  pl.BlockSpec((seq, W_A), lambda i: (i, 0)),
                  pl.BlockSpec((seq, 2 * LANES), lambda i: (i, 0)),
                  pl.BlockSpec(bias.shape, lambda i: (0, 0, 0, 0))],
        out_specs=pl.BlockSpec((seq, W_A), lambda i: (i, 0)),
        out_shape=jax.ShapeDtypeStruct((b * seq, W_A), BF16),
        compiler_params=_params(("parallel",)),
        name="attn_win",
    )(sink, qa, kva, bias)


def _attn_dil_kernel(qkv_ref, bias_ref, o_ref, l_ref, *, dil, length, half, kw):
    nblk = length // QBLK
    units, where = [], []
    for r in range(dil):
        for n in range(nblk):
            ws, var = _window(n, nblk, half, length, kw)
            for t in range(HB_PER_GROUP // 2):
                c = t * LANES

                def unit(r=r, n=n, c=c, t=t, ws=ws, var=var):
                    def score_fn():
                        q2 = qkv_ref[r, n * QBLK:(n + 1) * QBLK, c:c + LANES]
                        k2 = qkv_ref[r, ws:ws + kw, W_B_OUT + c:W_B_OUT + c + LANES]
                        return tuple(lax.dot_general(_half(q2, hi), k2, _NT, preferred_element_type=F32)
                                     + bias_ref[var, 2 * t + hi] for hi in (0, 1))

                    def value_fn(p_lo, p_hi):
                        v2 = qkv_ref[r, ws:ws + kw, 2 * W_B_OUT + c:2 * W_B_OUT + c + LANES]
                        o_lo = jnp.dot(p_lo, v2, preferred_element_type=F32)
                        o_hi = jnp.dot(p_hi, v2, preferred_element_type=F32)
                        return jnp.where(lax.broadcasted_iota(jnp.int32, o_lo.shape, 1) < HEAD_DIM, o_lo, o_hi)

                    return score_fn, value_fn, None

                units.append(unit)
                where.append((r, n, t))

    def emit(u, o, lse2):
        r, n, t = where[u]
        o_ref[t, r, n * QBLK:(n + 1) * QBLK, :] = o
        l_ref[t, r, n * QBLK:(n + 1) * QBLK, :] = lse2

    _pair_stream(units, emit, LOOKAHEAD_DIL if kw > QBLK else LOOKAHEAD)


def _attn_dil(qkv, bias, dil, half):
    b, _, length, _ = qkv.shape
    kw = bias.shape[-1]
    npair = HB_PER_GROUP // 2
    blk = (None, npair, dil, length, LANES)
    return pl.pallas_call(
        functools.partial(_attn_dil_kernel, dil=dil, length=length, half=half, kw=kw),
        grid=(b,),
        in_specs=[pl.BlockSpec((None, dil, length, 3 * W_B_OUT), lambda i: (i, 0, 0, 0)),
                  pl.BlockSpec(bias.shape, lambda i: (0, 0, 0, 0))],
        out_specs=[pl.BlockSpec(blk, lambda i: (i, 0, 0, 0, 0))] * 2,
        out_shape=[jax.ShapeDtypeStruct((b, npair, dil, length, LANES), F32)] * 2,
        compiler_params=_params(("parallel",)),
        name=f"attn_dil{dil}",
    )(qkv, bias)


def _attn_mem_kernel(q_ref, mkv_ref, o_ref, *, tq):
    scale = HEAD_DIM_M ** -0.5 * LOG2E
    for h in range(N_HEADS_M):
        c = h * HEAD_DIM_M
        mk = mkv_ref[:, c:c + HEAD_DIM_M]
        mv = mkv_ref[:, W_M + c:W_M + c + HEAD_DIM_M]
        for r in range(0, tq, QBLK):
            s = lax.dot_general(q_ref[r:r + QBLK, c:c + HEAD_DIM_M], mk, (((1,), (1,)), ((), ())),
                                preferred_element_type=F32) * scale
            m = jnp.max(s, axis=-1, keepdims=True)
            p = jnp.exp2(s - m)
            den = jnp.sum(p, axis=-1, keepdims=True)
            o = jnp.dot(p.astype(BF16), mv, preferred_element_type=F32) / den
            o_ref[r:r + QBLK, c:c + HEAD_DIM_M] = o.astype(o_ref.dtype)


def _attn_mem(qm, mkv, b, seq, n_mem, tq):
    nq = seq // tq
    return pl.pallas_call(
        functools.partial(_attn_mem_kernel, tq=tq),
        grid=(b, nq),
        in_specs=[pl.BlockSpec((tq, W_M), lambda i, j: (i * nq + j, 0)),
                  pl.BlockSpec((None, n_mem, 2 * W_M), lambda i, j: (i, 0, 0))],
        out_specs=pl.BlockSpec((tq, W_M), lambda i, j: (i * nq + j, 0)),
        out_shape=jax.ShapeDtypeStruct((b * seq, W_M), BF16),
        compiler_params=_params(("parallel", "arbitrary")),
        name="attn_mem",
    )(qm, mkv.reshape(b, n_mem, 2 * W_M))


def _tail_kernel(x_ref, oa_ref, om_ref, o0_ref, o1_ref, o2_ref, l0_ref, l1_ref, l2_ref,
                 g1_ref, wg_ref, wa_ref, wb_ref, wm_ref, wo_ref, g2_ref, wu_ref, wd_ref, gf_ref,
                 y_ref, mix_ref, *, final_norm):
    x = x_ref[...]
    h = _rms(x, g1_ref[...]).astype(BF16)

    def slabs(ref, k):
        npair, dil, rows, _ = ref.shape
        if dil == 1:
            return jnp.concatenate([ref[p, 0] for p in range(npair)], axis=1)
        for p in range(npair):
            for r in range(dil):
                mix_ref[k, p, pl.ds(r, rows, stride=dil), :] = ref[p, r]
        return jnp.concatenate([mix_ref[k, p] for p in range(npair)], axis=1)

    l0, l1, l2 = slabs(l0_ref, 0), slabs(l1_ref, 0), slabs(l2_ref, 1)
    mx = jnp.maximum(jnp.maximum(l0, l1), l2)
    e0, e1, e2 = jnp.exp2(l0 - mx), jnp.exp2(l1 - mx), jnp.exp2(l2 - mx)
    ob = (e0 * slabs(o0_ref, 0) + e1 * slabs(o1_ref, 2) + e2 * slabs(o2_ref, 3)) / (e0 + e1 + e2)

    def gate(k):
        return jax.nn.sigmoid(jnp.dot(h, wg_ref[:, k * D_MODEL:(k + 1) * D_MODEL],
                                      preferred_element_type=F32))

    merged = gate(0) * jnp.dot(oa_ref[...], wa_ref[...], preferred_element_type=F32)
    merged += gate(1) * jnp.dot(ob.astype(BF16), wb_ref[...], preferred_element_type=F32)
    merged += gate(2) * jnp.dot(om_ref[...], wm_ref[...], preferred_element_type=F32)
    x = x + jnp.dot(merged.astype(BF16), wo_ref[...], preferred_element_type=F32)

    h2 = _rms(x, g2_ref[...]).astype(BF16)
    ck = 1024
    for c in range(0, D_FF, ck):
        u = jnp.dot(h2, wu_ref[:, c:c + ck], preferred_element_type=F32)
        a = jnp.square(jnp.maximum(u, 0.0)).astype(BF16)
        x = x + jnp.dot(a, wd_ref[c:c + ck, :], preferred_element_type=F32)
    if final_norm:
        x = _rms(x, gf_ref[...])
    y_ref[...] = x


def _tail(x2d, oa, om, obs, lbs, g1, wg, wa, wb, wm, wo, g2, wu, wd, gf, tm, final_norm):
    t = x2d.shape[0]
    npair = HB_PER_GROUP // 2
    nt = obs[0].shape[2] * obs[0].shape[3] // tm

    def rows(width):
        return pl.BlockSpec((tm, width), lambda i: (i, 0))

    def whole(a):
        return pl.BlockSpec(a.shape, lambda i: (0, 0), pipeline_mode=pl.Buffered(1))

    def slab(a):
        dil = a.shape[2]
        return pl.BlockSpec((None, npair, dil, tm // dil, LANES), lambda i: (i // nt, 0, 0, i % nt, 0))

    consts = (g1, wg, wa, wb, wm, wo, g2, wu, wd, gf)
    return pl.pallas_call(
        functools.partial(_tail_kernel, final_norm=final_norm),
        grid=(t // tm,),
        in_specs=([rows(D_MODEL), rows(W_A), rows(W_M)] + [slab(a) for a in (*obs, *lbs)]
                  + [whole(a) for a in consts]),
        out_specs=rows(D_MODEL),
        out_shape=jax.ShapeDtypeStruct((t, D_MODEL), F32),
        scratch_shapes=[pltpu.VMEM((4, npair, tm, LANES), F32)],
        compiler_params=_params(("parallel",)),
        name="tail",
    )(x2d, oa, om, *obs, *lbs, *consts)


def _layer(x, mem, rel_bias, norm1_g, w_in, mem_norm_g, w_mem_kv, sink_logit, w_branch_a,
           w_branch_b, w_branch_m, w_out, norm2_g, w_up, w_down, final_norm_g, final_norm):
    b, seq, _ = x.shape
    n_mem = mem.shape[1]
    t = b * seq
    x2d = x.reshape(t, D_MODEL)

    off = np.cumsum((0,) + IN_SPLITS)
    half_heads = N_HEADS_A // 2
    scale = HEAD_DIM ** -0.5 * LOG2E
    w_qa = (w_in[:, off[0]:off[1]].reshape(D_MODEL, 2, half_heads, HEAD_DIM)
            .transpose(0, 2, 1, 3).reshape(D_MODEL, W_A)) * scale
    parts = [w_qa, w_in[:, off[1]:off[3]]]
    for g in range(len(DILATED_GROUPS)):
        lo, hi = g * W_B_OUT, (g + 1) * W_B_OUT
        parts += [w_in[:, off[3] + lo:off[3] + hi] * scale, w_in[:, off[4] + lo:off[4] + hi],
                  w_in[:, off[5] + lo:off[5] + hi]]
    parts.append(w_in[:, off[6]:off[7]])
    w_qkv = jnp.concatenate(parts, axis=1).astype(BF16)
    plan = ((W_A, 1), (2 * W_KV_A, 1)) + tuple((3 * W_B_OUT, dil) for _, dil in DILATED_GROUPS) + ((W_M, 1),)
    qa, kva, qkv0, qkv1, qkv2, qm = _in_proj(x, norm1_g.reshape(1, D_MODEL), w_qkv, plan, tm=512)

    bias_a = _bias_tiles(rel_bias, np.arange(N_HEADS_A), WIN_A, 1, 3 * WIN_A, (0, -WIN_A, -2 * WIN_A))
    oa = _attn_win(sink_logit.astype(F32) * LOG2E, qa, kva, bias_a, b, seq)

    obs, lbs = [], []
    for g, ((win, dil), qkv) in enumerate(zip(DILATED_GROUPS, (qkv0, qkv1, qkv2))):
        half = win // (2 * dil)
        length = seq // dil
        kw = min(4 * half, length)
        deltas = (0, -half, -2 * half) if length > kw else (0,)
        hcols = N_HEADS_A + g * HB_PER_GROUP + np.arange(HB_PER_GROUP)
        bias_g = _bias_tiles(rel_bias, hcols, half, dil, kw, deltas)
        o, l = _attn_dil(qkv.reshape(b, dil, length, 3 * W_B_OUT), bias_g, dil, half)
        obs.append(o)
        lbs.append(l)

    mkv, = _in_proj(mem.reshape(1, b * n_mem, D_MODEL), mem_norm_g.reshape(1, D_MODEL),
                    w_mem_kv.astype(BF16), ((2 * W_M, 1),), tm=512)
    om = _attn_mem(qm, mkv, b, seq, n_mem, tq=512)

    wa = (w_branch_a.reshape(2, half_heads, HEAD_DIM, D_MODEL).transpose(1, 0, 2, 3)
          .reshape(W_A, D_MODEL))
    y = _tail(x2d, oa, om, obs, lbs, norm1_g.reshape(1, D_MODEL), w_in[:, off[7]:].astype(BF16),
              wa.astype(BF16), w_branch_b.astype(BF16), w_branch_m.astype(BF16), w_out.astype(BF16),
              norm2_g.reshape(1, D_MODEL), w_up.astype(BF16), w_down.astype(BF16),
              final_norm_g.reshape(1, D_MODEL), tm=512, final_norm=final_norm)
    return y.reshape(b, seq, D_MODEL)


def _trunk(x, mem, rel_bias, norm1_g, w_in, mem_norm_g, w_mem_kv, sink_logit, w_branch_a,
           w_branch_b, w_branch_m, w_out, norm2_g, w_up, w_down, final_norm_g):
    depth = w_in.shape[0]
    for layer in range(depth):
        x = _layer(x, mem, rel_bias, norm1_g[layer], w_in[layer], mem_norm_g[layer], w_mem_kv[layer],
                   sink_logit[layer], w_branch_a[layer], w_branch_b[layer], w_branch_m[layer],
                   w_out[layer], norm2_g[layer], w_up[layer], w_down[layer], final_norm_g,
                   final_norm=layer == depth - 1)
    return x


def kernel(x_prompt, x_sample, mem_prompt, mem_sample, rel_bias, norm1_g, w_in, mem_norm_g, w_mem_kv,
           sink_logit, w_branch_a, w_branch_b, w_branch_m, w_out, norm2_g, w_up, w_down, final_norm_g):
    weights = (rel_bias, norm1_g, w_in, mem_norm_g, w_mem_kv, sink_logit, w_branch_a, w_branch_b,
               w_branch_m, w_out, norm2_g, w_up, w_down, final_norm_g)
    return (_trunk(x_prompt, mem_prompt, *weights), _trunk(x_sample, mem_sample, *weights))
```

```python
import functools

import numpy as np
import jax
import jax.numpy as jnp
from jax import lax
from jax.experimental import pallas as pl
from jax.experimental.pallas import tpu as pltpu

D_MODEL = 1024
HEAD_DIM = 64
N_HEADS_A = 8
N_KV_A = 2
WIN_A = 128
DILATED_GROUPS = ((128, 1), (512, 4), (2048, 16))
HB_PER_GROUP = 4
N_HEADS_B = HB_PER_GROUP * len(DILATED_GROUPS)
N_HEADS_M = 4
HEAD_DIM_M = 128
D_FF = 4 * D_MODEL
NUM_BUCKETS = 32
MAX_DIST = 1024
EPS = 1e-6
NEG = -1e30

W_A = N_HEADS_A * HEAD_DIM
W_KV_A = N_KV_A * HEAD_DIM
W_B = N_HEADS_B * HEAD_DIM
W_B_OUT = HB_PER_GROUP * HEAD_DIM
W_M = N_HEADS_M * HEAD_DIM_M
IN_SPLITS = (W_A, W_KV_A, W_KV_A, W_B, W_B, W_B, W_M, D_MODEL, D_MODEL, D_MODEL)

LANES = 128
QBLK = 128
LOOKAHEAD = 2
LOOKAHEAD_DIL = 3
LOG2E = 1.4426950408889634
FF_CHUNK = 1024
TAIL_CHAINS = 2
VMEM_LIMIT = 60 * 1024 * 1024
BF16 = jnp.bfloat16
F32 = jnp.float32


def _t5_bucket(rel):
    half = NUM_BUCKETS // 2
    ret = (rel > 0).astype(np.int32) * half
    n = np.abs(rel)
    max_exact = half // 2
    large = max_exact + (np.log(np.maximum(n, 1) / max_exact) / np.log(MAX_DIST / max_exact)
                         * (half - max_exact)).astype(np.int32)
    large = np.minimum(large, half - 1)
    return (ret + np.where(n < max_exact, n, large)).astype(np.int32)


def _bias_tiles(rel_bias, cols, half, dil, kw, deltas):
    tiles = []
    period = QBLK + kw
    k = np.arange(period)
    for delta in deltas:
        off = np.where(k < kw, k, k - period) + delta
        valid = np.abs(off) <= half
        bucket = _t5_bucket(dil * np.clip(off, -half, half))
        v = jnp.where(valid[None, :], rel_bias[bucket][:, cols].astype(F32).T * LOG2E, NEG)
        t = jnp.tile(v, (1, QBLK))[:, :QBLK * (period - 1)].reshape(len(cols), QBLK, period - 1)
        tiles.append(t[:, :, :kw])
    return jnp.stack(tiles)


def _params(sem):
    return pltpu.CompilerParams(dimension_semantics=sem, vmem_limit_bytes=VMEM_LIMIT)


def _rms(x, g):
    return x * lax.rsqrt(jnp.mean(x * x, axis=-1, keepdims=True) + EPS) * g


def _in_proj_kernel(x_ref, g_ref, w_ref, *refs, plan, tm):
    o_refs = refs[:len(plan)]
    h = _rms(x_ref[...], g_ref[...])
    hb = h.astype(BF16)
    if len(refs) > len(plan):
        hs_ref = refs[len(plan)]
        for j in range(D_MODEL // LANES):
            hs_ref[j] = h[:, j * LANES:(j + 1) * LANES]
    c = 0
    for o_ref, (wd, dil) in zip(o_refs, plan):
        rows = tm // dil
        if dil == 1:
            lhs = hb
        else:
            lhs = jnp.concatenate(
                [jnp.concatenate([hs_ref[j, pl.ds(r, rows, stride=dil), :]
                                  for j in range(D_MODEL // LANES)], axis=1)
                 for r in range(dil)], axis=0).astype(BF16)
        for s in range(0, wd, 512):
            e = min(wd, s + 512)
            res = jnp.dot(lhs, w_ref[:, c + s:c + e], preferred_element_type=F32).astype(o_ref.dtype)
            if dil == 1:
                o_ref[:, s:e] = res
            else:
                for r in range(dil):
                    o_ref[r, :, s:e] = res[r * rows:(r + 1) * rows]
        c += wd


def _in_proj(x, g, w, plan, tm):
    b, seq, _ = x.shape
    nt = seq // tm
    out_specs, out_shape = [], []
    for wd, dil in plan:
        if dil == 1:
            out_specs.append(pl.BlockSpec((tm, wd), lambda i, j: (i * nt + j, 0)))
            out_shape.append(jax.ShapeDtypeStruct((b * seq, wd), BF16))
        else:
            out_specs.append(pl.BlockSpec((None, dil, tm // dil, wd), lambda i, j: (i, 0, j, 0)))
            out_shape.append(jax.ShapeDtypeStruct((b, dil, seq // dil, wd), BF16))
    dilated = any(dil > 1 for _, dil in plan)
    return pl.pallas_call(
        functools.partial(_in_proj_kernel, plan=plan, tm=tm),
        grid=(b, nt),
        in_specs=[pl.BlockSpec((None, tm, D_MODEL), lambda i, j: (i, j, 0)),
                  pl.BlockSpec((1, D_MODEL), lambda i, j: (0, 0)),
                  pl.BlockSpec(w.shape, lambda i, j: (0, 0), pipeline_mode=pl.Buffered(1))],
        out_specs=out_specs,
        out_shape=out_shape,
        scratch_shapes=[pltpu.VMEM((D_MODEL // LANES, tm, LANES), F32)] if dilated else [],
        compiler_params=_params(("parallel", "arbitrary")),
        name="in_proj",
    )(x, g, w)


def _pair_stream(units, emit, lookahead):
    cache = {}

    def scores(u):
        cache[u] = units[u]()
        return cache[u][0]()

    def parts(s, sink):
        m = jnp.max(s, axis=-1, keepdims=True)
        if sink is not None:
            m = jnp.maximum(m, sink)
        p = jnp.exp2(s - m)
        return m, p.astype(BF16), jnp.sum(p, axis=-1, keepdims=True)

    pending = [scores(u) for u in range(min(lookahead, len(units)))]
    for u in range(len(units)):
        s_lo, s_hi = pending.pop(0)
        if u + lookahead < len(units):
            pending.append(scores(u + lookahead))
        _, value_fn, sinks = cache.pop(u)
        m_lo, p_lo, d_lo = parts(s_lo, sinks and sinks[0])
        m_hi, p_hi, d_hi = parts(s_hi, sinks and sinks[1])
        o = value_fn(p_lo, p_hi)
        low = lax.broadcasted_iota(jnp.int32, o.shape, 1) < HEAD_DIM
        m = jnp.where(low, m_lo, m_hi)
        den = jnp.where(low, d_lo, d_hi)
        if sinks:
            den = den + jnp.exp2(jnp.where(low, sinks[0], sinks[1]) - m)
        emit(u, o / den, m + jnp.log2(den))


def _half(x2, hi):
    lane = lax.broadcasted_iota(jnp.int32, x2.shape, 1)
    keep = lane >= HEAD_DIM if hi else lane < HEAD_DIM
    return jnp.where(keep, x2, jnp.zeros_like(x2))


def _window(n, nblk, half, length, kw):
    if nblk == 1:
        return 0, 0
    return min(max(n * QBLK - half, 0), length - kw), (0 if n == 0 else 2 if n == nblk - 1 else 1)


_NT = (((1,), (1,)), ((), ()))


def _attn_win_kernel(sink_ref, q_ref, kv_ref, bias_ref, o_ref, *, seq):
    nblk = seq // QBLK
    kw = 3 * WIN_A
    half_heads = N_HEADS_A // 2
    stacks = {}

    def stacked(n, ws):
        if n not in stacks:
            k2 = kv_ref[ws:ws + kw, 0:LANES]
            v2 = kv_ref[ws:ws + kw, LANES:2 * LANES]
            stacks.clear()
            stacks[n] = (jnp.concatenate([_half(k2, False), _half(k2, True)], axis=0),
                         jnp.concatenate([_half(v2, False), _half(v2, True)], axis=0))
        return stacks[n]

    units, where = [], []
    for n in range(nblk):
        ws, var = _window(n, nblk, WIN_A, seq, kw)
        for t in range(half_heads):

            def unit(n=n, t=t, ws=ws, var=var):
                km, vm = stacked(n, ws)

                def score_fn():
                    s = lax.dot_general(q_ref[n * QBLK:(n + 1) * QBLK, t * LANES:(t + 1) * LANES], km,
                                        _NT, preferred_element_type=F32)
                    return s[:, :kw] + bias_ref[var, t], s[:, kw:] + bias_ref[var, half_heads + t]

                def value_fn(p_lo, p_hi):
                    return jnp.dot(jnp.concatenate([p_lo, p_hi], axis=1), vm, preferred_element_type=F32)

                return score_fn, value_fn, (sink_ref[t], sink_ref[half_heads + t])

            units.append(unit)
            where.append((n, t))

    def emit(u, o, _):
        n, t = where[u]
        o_ref[n * QBLK:(n + 1) * QBLK, t * LANES:(t + 1) * LANES] = o.astype(o_ref.dtype)

    _pair_stream(units, emit, LOOKAHEAD)


def _attn_win(sink, qa, kva, bias, b, seq):
    return pl.pallas_call(
        functools.partial(_attn_win_kernel, seq=seq),
        grid=(b,),
        in_specs=[pl.BlockSpec(memory_space=pltpu.SMEM),
                  pl.BlockSpec((seq, W_A), lambda i: (i, 0)),
                  pl.BlockSpec((seq, 2 * LANES), lambda i: (i, 0)),
                  pl.BlockSpec(bias.shape, lambda i: (0, 0, 0, 0))],
        out_specs=pl.BlockSpec((seq, W_A), lambda i: (i, 0)),
        out_shape=jax.ShapeDtypeStruct((b * seq, W_A), BF16),
        compiler_params=_params(("parallel",)),
        name="attn_win",
    )(sink, qa, kva, bias)


def _attn_dil_kernel(qkv_ref, bias_ref, o_ref, l_ref, *, dil, length, half, kw):
    nblk = length // QBLK
    units, where = [], []
    for r in range(dil):
        for n in range(nblk):
            ws, var = _window(n, nblk, half, length, kw)
            for t in range(HB_PER_GROUP // 2):
                c = t * LANES

                def unit(r=r, n=n, c=c, t=t, ws=ws, var=var):
                    def score_fn():
                        q2 = qkv_ref[r, n * QBLK:(n + 1) * QBLK, c:c + LANES]
                        k2 = qkv_ref[r, ws:ws + kw, W_B_OUT + c:W_B_OUT + c + LANES]
                        return tuple(lax.dot_general(_half(q2, hi), k2, _NT, preferred_element_type=F32)
                                     + bias_ref[var, 2 * t + hi] for hi in (0, 1))

                    def value_fn(p_lo, p_hi):
                        v2 = qkv_ref[r, ws:ws + kw, 2 * W_B_OUT + c:2 * W_B_OUT + c + LANES]
                        o_lo = jnp.dot(p_lo, v2, preferred_element_type=F32)
                        o_hi = jnp.dot(p_hi, v2, preferred_element_type=F32)
                        return jnp.where(lax.broadcasted_iota(jnp.int32, o_lo.shape, 1) < HEAD_DIM, o_lo, o_hi)

                    return score_fn, value_fn, None

                units.append(unit)
                where.append((r, n, t))

    def emit(u, o, lse2):
        r, n, t = where[u]
        o_ref[t, r, n * QBLK:(n + 1) * QBLK, :] = o
        l_ref[t, r, n * QBLK:(n + 1) * QBLK, :] = lse2

    _pair_stream(units, emit, LOOKAHEAD_DIL if kw > QBLK else LOOKAHEAD)


def _attn_dil(qkv, bias, dil, half):
    b, _, length, _ = qkv.shape
    kw = bias.shape[-1]
    npair = HB_PER_GROUP // 2
    blk = (None, npair, dil, length, LANES)
    return pl.pallas_call(
        functools.partial(_attn_dil_kernel, dil=dil, length=length, half=half, kw=kw),
        grid=(b,),
        in_specs=[pl.BlockSpec((None, dil, length, 3 * W_B_OUT), lambda i: (i, 0, 0, 0)),
                  pl.BlockSpec(bias.shape, lambda i: (0, 0, 0, 0))],
        out_specs=[pl.BlockSpec(blk, lambda i: (i, 0, 0, 0, 0))] * 2,
        out_shape=[jax.ShapeDtypeStruct((b, npair, dil, length, LANES), F32)] * 2,
        compiler_params=_params(("parallel",)),
        name=f"attn_dil{dil}",
    )(qkv, bias)


def _attn_mem_kernel(q_ref, mkv_ref, o_ref, *, seq):
    units = [(r, h * HEAD_DIM_M) for r in range(0, seq, QBLK) for h in range(N_HEADS_M)]

    def scores(u):
        r, c = units[u]
        return lax.dot_general(q_ref[r:r + QBLK, c:c + HEAD_DIM_M], mkv_ref[:, c:c + HEAD_DIM_M], _NT,
                               preferred_element_type=F32)

    pending = [scores(u) for u in range(LOOKAHEAD)]
    for u, (r, c) in enumerate(units):
        s = pending.pop(0)
        if u + LOOKAHEAD < len(units):
            pending.append(scores(u + LOOKAHEAD))
        p = jnp.exp2(s - jnp.max(s, axis=-1, keepdims=True))
        den = jnp.sum(p, axis=-1, keepdims=True)
        o = jnp.dot(p.astype(BF16), mkv_ref[:, W_M + c:W_M + c + HEAD_DIM_M], preferred_element_type=F32)
        o_ref[r:r + QBLK, c:c + HEAD_DIM_M] = (o / den).astype(o_ref.dtype)


def _attn_mem(qm, mkv, b, seq, n_mem):
    return pl.pallas_call(
        functools.partial(_attn_mem_kernel, seq=seq),
        grid=(b,),
        in_specs=[pl.BlockSpec((seq, W_M), lambda i: (i, 0)),
                  pl.BlockSpec((None, n_mem, 2 * W_M), lambda i: (i, 0, 0))],
        out_specs=pl.BlockSpec((seq, W_M), lambda i: (i, 0)),
        out_shape=jax.ShapeDtypeStruct((b * seq, W_M), BF16),
        compiler_params=_params(("parallel",)),
        name="attn_mem",
    )(qm, mkv.reshape(b, n_mem, 2 * W_M))


def _tail_kernel(x_ref, oa_ref, om_ref, o0_ref, o1_ref, o2_ref, l0_ref, l1_ref, l2_ref,
                 g1_ref, wg_ref, wa_ref, wb_ref, wm_ref, wo_ref, g2_ref, wu_ref, wd_ref, gf_ref,
                 y_ref, mix_ref, *, final_norm):
    tm = x_ref.shape[0]
    half = tm // TAIL_CHAINS

    def dot(a, b):
        return jnp.dot(a, b, preferred_element_type=F32)

    def chain(j):
        rows = slice(j * half, (j + 1) * half)

        def slabs(ref, k):
            npair, dil, per, _ = ref.shape
            per //= TAIL_CHAINS
            if dil == 1:
                return jnp.concatenate([ref[p, 0, rows] for p in range(npair)], axis=1)
            for p in range(npair):
                for r in range(dil):
                    mix_ref[k, p, pl.ds(j * half + r, per, stride=dil), :] = ref[p, r, j * per:(j + 1) * per]
            return jnp.concatenate([mix_ref[k, p, rows] for p in range(npair)], axis=1)

        br_a = dot(oa_ref[rows], wa_ref[...])
        br_m = dot(om_ref[rows], wm_ref[...])
        yield
        x = x_ref[rows]
        h = _rms(x, g1_ref[...]).astype(BF16)
        g_a, g_b, g_m = (dot(h, wg_ref[:, k * D_MODEL:(k + 1) * D_MODEL]) for k in range(3))
        yield
        l0, l1, l2 = slabs(l0_ref, 0), slabs(l1_ref, 0), slabs(l2_ref, 1)
        mx = jnp.maximum(jnp.maximum(l0, l1), l2)
        e0, e1, e2 = jnp.exp2(l0 - mx), jnp.exp2(l1 - mx), jnp.exp2(l2 - mx)
        ob = (e0 * slabs(o0_ref, 0) + e1 * slabs(o1_ref, 2) + e2 * slabs(o2_ref, 3)) / (e0 + e1 + e2)
        br_b = dot(ob.astype(BF16), wb_ref[...])
        yield
        merged = jax.nn.sigmoid(g_a) * br_a + jax.nn.sigmoid(g_b) * br_b + jax.nn.sigmoid(g_m) * br_m
        x = x + dot(merged.astype(BF16), wo_ref[...])
        yield
        h2 = _rms(x, g2_ref[...]).astype(BF16)
        for c in range(D_FF // FF_CHUNK):
            u = dot(h2, wu_ref[:, c * FF_CHUNK:(c + 1) * FF_CHUNK])
            yield
            a = jnp.square(jnp.maximum(u, 0.0)).astype(BF16)
            x = x + dot(a, wd_ref[c * FF_CHUNK:(c + 1) * FF_CHUNK, :])
            yield
        if final_norm:
            x = _rms(x, gf_ref[...])
        y_ref[rows] = x

    done = object()
    chains = [chain(j) for j in range(TAIL_CHAINS)]
    while chains:
        chains = [g for g in chains if next(g, done) is not done]


def _tail(x2d, oa, om, obs, lbs, g1, wg, wa, wb, wm, wo, g2, wu, wd, gf, tm, final_norm):
    t = x2d.shape[0]
    npair = HB_PER_GROUP // 2
    nt = obs[0].shape[2] * obs[0].shape[3] // tm

    def rows(width):
        return pl.BlockSpec((tm, width), lambda i: (i, 0))

    def whole(a):
        return pl.BlockSpec(a.shape, lambda i: (0, 0), pipeline_mode=pl.Buffered(1))

    def slab(a):
        dil = a.shape[2]
        return pl.BlockSpec((None, npair, dil, tm // dil, LANES), lambda i: (i // nt, 0, 0, i % nt, 0))

    consts = (g1, wg, wa, wb, wm, wo, g2, wu, wd, gf)
    return pl.pallas_call(
        functools.partial(_tail_kernel, final_norm=final_norm),
        grid=(t // tm,),
        in_specs=([rows(D_MODEL), rows(W_A), rows(W_M)] + [slab(a) for a in (*obs, *lbs)]
                  + [whole(a) for a in consts]),
        out_specs=rows(D_MODEL),
        out_shape=jax.ShapeDtypeStruct((t, D_MODEL), F32),
        scratch_shapes=[pltpu.VMEM((4, npair, tm, LANES), F32)],
        compiler_params=_params(("parallel",)),
        name="tail",
    )(x2d, oa, om, *obs, *lbs, *consts)


def _layer(x, mem, rel_bias, norm1_g, w_in, mem_norm_g, w_mem_kv, sink_logit, w_branch_a,
           w_branch_b, w_branch_m, w_out, norm2_g, w_up, w_down, final_norm_g, final_norm):
    b, seq, _ = x.shape
    n_mem = mem.shape[1]
    t = b * seq
    x2d = x.reshape(t, D_MODEL)

    off = np.cumsum((0,) + IN_SPLITS)
    half_heads = N_HEADS_A // 2
    scale = HEAD_DIM ** -0.5 * LOG2E
    w_qa = (w_in[:, off[0]:off[1]].reshape(D_MODEL, 2, half_heads, HEAD_DIM)
            .transpose(0, 2, 1, 3).reshape(D_MODEL, W_A)) * scale
    parts = [w_qa, w_in[:, off[1]:off[3]]]
    for g in range(len(DILATED_GROUPS)):
        lo, hi = g * W_B_OUT, (g + 1) * W_B_OUT
        parts += [w_in[:, off[3] + lo:off[3] + hi] * scale, w_in[:, off[4] + lo:off[4] + hi],
                  w_in[:, off[5] + lo:off[5] + hi]]
    parts.append(w_in[:, off[6]:off[7]] * (HEAD_DIM_M ** -0.5 * LOG2E))
    w_qkv = jnp.concatenate(parts, axis=1).astype(BF16)
    plan = ((W_A, 1), (2 * W_KV_A, 1)) + tuple((3 * W_B_OUT, dil) for _, dil in DILATED_GROUPS) + ((W_M, 1),)
    qa, kva, qkv0, qkv1, qkv2, qm = _in_proj(x, norm1_g.reshape(1, D_MODEL), w_qkv, plan, tm=1024)

    bias_a = _bias_tiles(rel_bias, np.arange(N_HEADS_A), WIN_A, 1, 3 * WIN_A, (0, -WIN_A, -2 * WIN_A))
    oa = _attn_win(sink_logit.astype(F32) * LOG2E, qa, kva, bias_a, b, seq)

    obs, lbs = [], []
    for g, ((win, dil), qkv) in enumerate(zip(DILATED_GROUPS, (qkv0, qkv1, qkv2))):
        half = win // (2 * dil)
        length = seq // dil
        kw = min(4 * half, length)
        deltas = (0, -half, -2 * half) if length > kw else (0,)
        hcols = N_HEADS_A + g * HB_PER_GROUP + np.arange(HB_PER_GROUP)
        bias_g = _bias_tiles(rel_bias, hcols, half, dil, kw, deltas)
        o, l = _attn_dil(qkv.reshape(b, dil, length, 3 * W_B_OUT), bias_g, dil, half)
        obs.append(o)
        lbs.append(l)

    mkv, = _in_proj(mem.reshape(1, b * n_mem, D_MODEL), mem_norm_g.reshape(1, D_MODEL),
                    w_mem_kv.astype(BF16), ((2 * W_M, 1),), tm=512)
    om = _attn_mem(qm, mkv, b, seq, n_mem)

    wa = (w_branch_a.reshape(2, half_heads, HEAD_DIM, D_MODEL).transpose(1, 0, 2, 3)
          .reshape(W_A, D_MODEL))
    y = _tail(x2d, oa, om, obs, lbs, norm1_g.reshape(1, D_MODEL), w_in[:, off[7]:].astype(BF16),
              wa.astype(BF16), w_branch_b.astype(BF16), w_branch_m.astype(BF16), w_out.astype(BF16),
              norm2_g.reshape(1, D_MODEL), w_up.astype(BF16), w_down.astype(BF16),
              final_norm_g.reshape(1, D_MODEL), tm=512, final_norm=final_norm)
    return y.reshape(b, seq, D_MODEL)


def _trunk(x, mem, rel_bias, norm1_g, w_in, mem_norm_g, w_mem_kv, sink_logit, w_branch_a,
           w_branch_b, w_branch_m, w_out, norm2_g, w_up, w_down, final_norm_g):
    depth = w_in.shape[0]
    for layer in range(depth):
        x = _layer(x, mem, rel_bias, norm1_g[layer], w_in[layer], mem_norm_g[layer], w_mem_kv[layer],
                   sink_logit[layer], w_branch_a[layer], w_branch_b[layer], w_branch_m[layer],
                   w_out[layer], norm2_g[layer], w_up[layer], w_down[layer], final_norm_g,
                   final_norm=layer == depth - 1)
    return x


def kernel(x_prompt, x_sample, mem_prompt, mem_sample, rel_bias, norm1_g, w_in, mem_norm_g, w_mem_kv,
           sink_logit, w_branch_a, w_branch_b, w_branch_m, w_out, norm2_g, w_up, w_down, final_norm_g):
    weights = (rel_bias, norm1_g, w_in, mem_norm_g, w_mem_kv, sink_logit, w_branch_a, w_branch_b,
               w_branch_m, w_out, norm2_g, w_up, w_down, final_norm_g)
    return (_trunk(x_prompt, mem_prompt, *weights), _trunk(x_sample, mem_sample, *weights))
```

```python
import functools

import numpy as np
import jax
import jax.numpy as jnp
from jax import lax
from jax.experimental import pallas as pl
from jax.experimental.pallas import tpu as pltpu

D_MODEL = 1024
HEAD_DIM = 64
N_HEADS_A = 8
N_KV_A = 2
WIN_A = 128
DILATED_GROUPS = ((128, 1), (512, 4), (2048, 16))
HB_PER_GROUP = 4
N_HEADS_B = HB_PER_GROUP * len(DILATED_GROUPS)
N_HEADS_M = 4
HEAD_DIM_M = 128
D_FF = 4 * D_MODEL
NUM_BUCKETS = 32
MAX_DIST = 1024
EPS = 1e-6
NEG = -1e30

W_A = N_HEADS_A * HEAD_DIM
W_KV_A = N_KV_A * HEAD_DIM
W_B = N_HEADS_B * HEAD_DIM
W_B_OUT = HB_PER_GROUP * HEAD_DIM
W_M = N_HEADS_M * HEAD_DIM_M
IN_SPLITS = (W_A, W_KV_A, W_KV_A, W_B, W_B, W_B, W_M, D_MODEL, D_MODEL, D_MODEL)

LANES = 128
QBLK = 128
LOOKAHEAD = 2
LOOKAHEAD_DIL = 3
LOG2E = 1.4426950408889634
FF_CHUNK = 2048
TAIL_CHAINS = 2
VMEM_LIMIT = 60 * 1024 * 1024
BF16 = jnp.bfloat16
F32 = jnp.float32


def _t5_bucket(rel):
    half = NUM_BUCKETS // 2
    ret = (rel > 0).astype(np.int32) * half
    n = np.abs(rel)
    max_exact = half // 2
    large = max_exact + (np.log(np.maximum(n, 1) / max_exact) / np.log(MAX_DIST / max_exact)
                         * (half - max_exact)).astype(np.int32)
    large = np.minimum(large, half - 1)
    return (ret + np.where(n < max_exact, n, large)).astype(np.int32)


def _bias_tiles(rel_bias, cols, half, dil, kw, deltas):
    tiles = []
    period = QBLK + kw
    k = np.arange(period)
    for delta in deltas:
        off = np.where(k < kw, k, k - period) + delta
        valid = np.abs(off) <= half
        bucket = _t5_bucket(dil * np.clip(off, -half, half))
        v = jnp.where(valid[None, :], rel_bias[bucket][:, cols].astype(F32).T * LOG2E, NEG)
        t = jnp.tile(v, (1, QBLK))[:, :QBLK * (period - 1)].reshape(len(cols), QBLK, period - 1)
        tiles.append(t[:, :, :kw])
    return jnp.stack(tiles)


def _params(sem, flags=None):
    return pltpu.CompilerParams(dimension_semantics=sem, vmem_limit_bytes=VMEM_LIMIT, flags=flags)


def _rms(x, g):
    return x * lax.rsqrt(jnp.mean(x * x, axis=-1, keepdims=True) + EPS) * g


def _in_proj_kernel(x_ref, g_ref, w_ref, *refs, plan, tm, chains):
    o_refs = refs[:len(plan)]
    hs_ref = refs[len(plan)] if len(refs) > len(plan) else None
    half = tm // chains

    def chain(j):
        base = j * half
        h = _rms(x_ref[base:base + half], g_ref[...])
        hb = h.astype(BF16)
        if hs_ref is not None:
            for k in range(D_MODEL // LANES):
                hs_ref[k, base:base + half] = h[:, k * LANES:(k + 1) * LANES]
        c = 0
        for o_ref, (wd, dil) in zip(o_refs, plan):
            rows = half // dil
            if dil == 1:
                lhs = hb
            else:
                lhs = jnp.concatenate(
                    [jnp.concatenate([hs_ref[k, pl.ds(base + r, rows, stride=dil), :]
                                      for k in range(D_MODEL // LANES)], axis=1)
                     for r in range(dil)], axis=0).astype(BF16)
            for s in range(0, wd, 512):
                e = min(wd, s + 512)
                res = jnp.dot(lhs, w_ref[:, c + s:c + e], preferred_element_type=F32).astype(o_ref.dtype)
                yield
                if dil == 1:
                    o_ref[base:base + half, s:e] = res
                else:
                    for r in range(dil):
                        o_ref[r, j * rows:(j + 1) * rows, s:e] = res[r * rows:(r + 1) * rows]
            c += wd

    done = object()
    live = [chain(j) for j in range(chains)]
    while live:
        live = [g for g in live if next(g, done) is not done]


def _in_proj(x, g, w, plan, tm):
    b, seq, _ = x.shape
    nt = seq // tm
    out_specs, out_shape = [], []
    for wd, dil in plan:
        if dil == 1:
            out_specs.append(pl.BlockSpec((tm, wd), lambda i, j: (i * nt + j, 0)))
            out_shape.append(jax.ShapeDtypeStruct((b * seq, wd), BF16))
        else:
            out_specs.append(pl.BlockSpec((None, dil, tm // dil, wd), lambda i, j: (i, 0, j, 0)))
            out_shape.append(jax.ShapeDtypeStruct((b, dil, seq // dil, wd), BF16))
    dilated = any(dil > 1 for _, dil in plan)
    return pl.pallas_call(
        functools.partial(_in_proj_kernel, plan=plan, tm=tm, chains=tm // 512),
        grid=(b, nt),
        in_specs=[pl.BlockSpec((None, tm, D_MODEL), lambda i, j: (i, j, 0)),
                  pl.BlockSpec((1, D_MODEL), lambda i, j: (0, 0)),
                  pl.BlockSpec(w.shape, lambda i, j: (0, 0), pipeline_mode=pl.Buffered(1))],
        out_specs=out_specs,
        out_shape=out_shape,
        scratch_shapes=[pltpu.VMEM((D_MODEL // LANES, tm, LANES), F32)] if dilated else [],
        compiler_params=_params(("parallel", "arbitrary")),
        name="in_proj",
    )(x, g, w)


def _pair_stream(units, emit, lookahead):
    cache = {}

    def scores(u):
        cache[u] = units[u]()
        return cache[u][0]()

    def parts(s, sink):
        m = jnp.max(s, axis=-1, keepdims=True)
        if sink is not None:
            m = jnp.maximum(m, sink)
        p = jnp.exp2(s - m)
        return m, p.astype(BF16), jnp.sum(p, axis=-1, keepdims=True)

    pending = [scores(u) for u in range(min(lookahead, len(units)))]
    for u in range(len(units)):
        s_lo, s_hi = pending.pop(0)
        if u + lookahead < len(units):
            pending.append(scores(u + lookahead))
        _, value_fn, sinks = cache.pop(u)
        m_lo, p_lo, d_lo = parts(s_lo, sinks and sinks[0])
        m_hi, p_hi, d_hi = parts(s_hi, sinks and sinks[1])
        o = value_fn(p_lo, p_hi)
        low = lax.broadcasted_iota(jnp.int32, o.shape, 1) < HEAD_DIM
        m = jnp.where(low, m_lo, m_hi)
        den = jnp.where(low, d_lo, d_hi)
        if sinks:
            den = den + jnp.exp2(jnp.where(low, sinks[0], sinks[1]) - m)
        emit(u, o / den, m + jnp.log2(den))


def _half(x2, hi):
    lane = lax.broadcasted_iota(jnp.int32, x2.shape, 1)
    keep = lane >= HEAD_DIM if hi else lane < HEAD_DIM
    return jnp.where(keep, x2, jnp.zeros_like(x2))


def _window(n, nblk, half, length, kw):
    if nblk == 1:
        return 0, 0
    return min(max(n * QBLK - half, 0), length - kw), (0 if n == 0 else 2 if n == nblk - 1 else 1)


_NT = (((1,), (1,)), ((), ()))


def _attn_win_kernel(sink_ref, q_ref, kv_ref, bias_ref, o_ref, *, seq):
    nblk = seq // QBLK
    kw = 3 * WIN_A
    half_heads = N_HEADS_A // 2
    stacks = {}

    def stacked(n, ws):
        if n not in stacks:
            k2 = kv_ref[ws:ws + kw, 0:LANES]
            v2 = kv_ref[ws:ws + kw, LANES:2 * LANES]
            stacks.clear()
            stacks[n] = (jnp.concatenate([_half(k2, False), _half(k2, True)], axis=0),
                         jnp.concatenate([_half(v2, False), _half(v2, True)], axis=0))
        return stacks[n]

    units, where = [], []
    for n in range(nblk):
        ws, var = _window(n, nblk, WIN_A, seq, kw)
        for t in range(half_heads):

            def unit(n=n, t=t, ws=ws, var=var):
                km, vm = stacked(n, ws)

                def score_fn():
                    s = lax.dot_general(q_ref[n * QBLK:(n + 1) * QBLK, t * LANES:(t + 1) * LANES], km,
                                        _NT, preferred_element_type=F32)
                    return s[:, :kw] + bias_ref[var, t], s[:, kw:] + bias_ref[var, half_heads + t]

                def value_fn(p_lo, p_hi):
                    return jnp.dot(jnp.concatenate([p_lo, p_hi], axis=1), vm, preferred_element_type=F32)

                return score_fn, value_fn, (sink_ref[t], sink_ref[half_heads + t])

            units.append(unit)
            where.append((n, t))

    def emit(u, o, _):
        n, t = where[u]
        o_ref[n * QBLK:(n + 1) * QBLK, t * LANES:(t + 1) * LANES] = o.astype(o_ref.dtype)

    _pair_stream(units, emit, LOOKAHEAD)


def _attn_win(sink, qa, kva, bias, b, seq):
    return pl.pallas_call(
        functools.partial(_attn_win_kernel, seq=seq),
        grid=(b,),
        in_specs=[pl.BlockSpec(memory_space=pltpu.SMEM),
                  pl.BlockSpec((seq, W_A), lambda i: (i, 0)),
                  pl.BlockSpec((seq, 2 * LANES), lambda i: (i, 0)),
                  pl.BlockSpec(bias.shape, lambda i: (0, 0, 0, 0))],
        out_specs=pl.BlockSpec((seq, W_A), lambda i: (i, 0)),
        out_shape=jax.ShapeDtypeStruct((b * seq, W_A), BF16),
        compiler_params=_params(("parallel",)),
        name="attn_win",
    )(sink, qa, kva, bias)


def _attn_dil_kernel(qkv_ref, bias_ref, o_ref, l_ref, *, dil, length, half, kw):
    nblk = length // QBLK
    units, where = [], []
    for r in range(dil):
        for n in range(nblk):
            ws, var = _window(n, nblk, half, length, kw)
            for t in range(HB_PER_GROUP // 2):
                c = t * LANES

                def unit(r=r, n=n, c=c, t=t, ws=ws, var=var):
                    def score_fn():
                        q2 = qkv_ref[r, n * QBLK:(n + 1) * QBLK, c:c + LANES]
                        k2 = qkv_ref[r, ws:ws + kw, W_B_OUT + c:W_B_OUT + c + LANES]
                        return tuple(lax.dot_general(_half(q2, hi), k2, _NT, preferred_element_type=F32)
                                     + bias_ref[var, 2 * t + hi] for hi in (0, 1))

                    def value_fn(p_lo, p_hi):
                        v2 = qkv_ref[r, ws:ws + kw, 2 * W_B_OUT + c:2 * W_B_OUT + c + LANES]
                        o_lo = jnp.dot(p_lo, v2, preferred_element_type=F32)
                        o_hi = jnp.dot(p_hi, v2, preferred_element_type=F32)
                        return jnp.where(lax.broadcasted_iota(jnp.int32, o_lo.shape, 1) < HEAD_DIM, o_lo, o_hi)

                    return score_fn, value_fn, None

                units.append(unit)
                where.append((r, n, t))

    def emit(u, o, lse2):
        r, n, t = where[u]
        o_ref[t, r, n * QBLK:(n + 1) * QBLK, :] = o
        l_ref[t, r, n * QBLK:(n + 1) * QBLK, :] = lse2

    _pair_stream(units, emit, LOOKAHEAD_DIL if kw > QBLK else LOOKAHEAD)


def _attn_dil(qkv, bias, dil, half):
    b, _, length, _ = qkv.shape
    kw = bias.shape[-1]
    npair = HB_PER_GROUP // 2
    blk = (None, npair, dil, length, LANES)
    return pl.pallas_call(
        functools.partial(_attn_dil_kernel, dil=dil, length=length, half=half, kw=kw),
        grid=(b,),
        in_specs=[pl.BlockSpec((None, dil, length, 3 * W_B_OUT), lambda i: (i, 0, 0, 0)),
                  pl.BlockSpec(bias.shape, lambda i: (0, 0, 0, 0))],
        out_specs=[pl.BlockSpec(blk, lambda i: (i, 0, 0, 0, 0))] * 2,
        out_shape=[jax.ShapeDtypeStruct((b, npair, dil, length, LANES), F32)] * 2,
        compiler_params=_params(("parallel",)),
        name=f"attn_dil{dil}",
    )(qkv, bias)


def _attn_mem_kernel(q_ref, mkv_ref, o_ref, *, seq):
    units = [(r, h * HEAD_DIM_M) for r in range(0, seq, QBLK) for h in range(N_HEADS_M)]

    def scores(u):
        r, c = units[u]
        return lax.dot_general(q_ref[r:r + QBLK, c:c + HEAD_DIM_M], mkv_ref[:, c:c + HEAD_DIM_M], _NT,
                               preferred_element_type=F32)

    pending = [scores(u) for u in range(LOOKAHEAD)]
    for u, (r, c) in enumerate(units):
        s = pending.pop(0)
        if u + LOOKAHEAD < len(units):
            pending.append(scores(u + LOOKAHEAD))
        p = jnp.exp2(s - jnp.max(s, axis=-1, keepdims=True))
        den = jnp.sum(p, axis=-1, keepdims=True)
        o = jnp.dot(p.astype(BF16), mkv_ref[:, W_M + c:W_M + c + HEAD_DIM_M], preferred_element_type=F32)
        o_ref[r:r + QBLK, c:c + HEAD_DIM_M] = (o / den).astype(o_ref.dtype)


def _attn_mem(qm, mkv, b, seq, n_mem):
    return pl.pallas_call(
        functools.partial(_attn_mem_kernel, seq=seq),
        grid=(b,),
        in_specs=[pl.BlockSpec((seq, W_M), lambda i: (i, 0)),
                  pl.BlockSpec((None, n_mem, 2 * W_M), lambda i: (i, 0, 0))],
        out_specs=pl.BlockSpec((seq, W_M), lambda i: (i, 0)),
        out_shape=jax.ShapeDtypeStruct((b * seq, W_M), BF16),
        compiler_params=_params(("parallel",)),
        name="attn_mem",
    )(qm, mkv.reshape(b, n_mem, 2 * W_M))


def _tail_kernel(x_ref, oa_ref, om_ref, o0_ref, o1_ref, o2_ref, l0_ref, l1_ref, l2_ref,
                 g1_ref, wg_ref, wa_ref, wb_ref, wm_ref, wo_ref, g2_ref, wu_ref, wd_ref, gf_ref,
                 y_ref, mix_ref, *, final_norm):
    tm = x_ref.shape[0]
    half = tm // TAIL_CHAINS

    def dot(a, b):
        return jnp.dot(a, b, preferred_element_type=F32)

    def chain(j):
        rows = slice(j * half, (j + 1) * half)

        def slabs(ref, k):
            npair, dil, per, _ = ref.shape
            per //= TAIL_CHAINS
            if dil == 1:
                return jnp.concatenate([ref[p, 0, rows] for p in range(npair)], axis=1)
            for p in range(npair):
                for r in range(dil):
                    mix_ref[k, p, pl.ds(j * half + r, per, stride=dil), :] = ref[p, r, j * per:(j + 1) * per]
            return jnp.concatenate([mix_ref[k, p, rows] for p in range(npair)], axis=1)

        br_a = dot(oa_ref[rows], wa_ref[...])
        br_m = dot(om_ref[rows], wm_ref[...])
        yield
        x = x_ref[rows]
        h = _rms(x, g1_ref[...]).astype(BF16)
        g_a, g_b, g_m = (dot(h, wg_ref[:, k * D_MODEL:(k + 1) * D_MODEL]) for k in range(3))
        yield
        l0, l1, l2 = slabs(l0_ref, 0), slabs(l1_ref, 0), slabs(l2_ref, 1)
        mx = jnp.maximum(jnp.maximum(l0, l1), l2)
        e0, e1, e2 = jnp.exp2(l0 - mx), jnp.exp2(l1 - mx), jnp.exp2(l2 - mx)
        ob = (e0 * slabs(o0_ref, 0) + e1 * slabs(o1_ref, 2) + e2 * slabs(o2_ref, 3)) / (e0 + e1 + e2)
        br_b = dot(ob.astype(BF16), wb_ref[...])
        yield
        merged = jax.nn.sigmoid(g_a) * br_a + jax.nn.sigmoid(g_b) * br_b + jax.nn.sigmoid(g_m) * br_m
        x = x + dot(merged.astype(BF16), wo_ref[...])
        yield
        h2 = _rms(x, g2_ref[...]).astype(BF16)
        for c in range(D_FF // FF_CHUNK):
            u = dot(h2, wu_ref[:, c * FF_CHUNK:(c + 1) * FF_CHUNK])
            yield
            a = jnp.square(jnp.maximum(u, 0.0)).astype(BF16)
            x = x + dot(a, wd_ref[c * FF_CHUNK:(c + 1) * FF_CHUNK, :])
            yield
        if final_norm:
            x = _rms(x, gf_ref[...])
        y_ref[rows] = x

    done = object()
    chains = [chain(j) for j in range(TAIL_CHAINS)]
    while chains:
        chains = [g for g in chains if next(g, done) is not done]


def _tail(x2d, oa, om, obs, lbs, g1, wg, wa, wb, wm, wo, g2, wu, wd, gf, tm, final_norm):
    t = x2d.shape[0]
    npair = HB_PER_GROUP // 2
    nt = obs[0].shape[2] * obs[0].shape[3] // tm

    def rows(width):
        return pl.BlockSpec((tm, width), lambda i: (i, 0))

    def whole(a):
        return pl.BlockSpec(a.shape, lambda i: (0, 0), pipeline_mode=pl.Buffered(1))

    def slab(a):
        dil = a.shape[2]
        return pl.BlockSpec((None, npair, dil, tm // dil, LANES), lambda i: (i // nt, 0, 0, i % nt, 0))

    consts = (g1, wg, wa, wb, wm, wo, g2, wu, wd, gf)
    return pl.pallas_call(
        functools.partial(_tail_kernel, final_norm=final_norm),
        grid=(t // tm,),
        in_specs=([rows(D_MODEL), rows(W_A), rows(W_M)] + [slab(a) for a in (*obs, *lbs)]
                  + [whole(a) for a in consts]),
        out_specs=rows(D_MODEL),
        out_shape=jax.ShapeDtypeStruct((t, D_MODEL), F32),
        scratch_shapes=[pltpu.VMEM((4, npair, tm, LANES), F32)],
        compiler_params=_params(("parallel",)),
        name="tail",
    )(x2d, oa, om, *obs, *lbs, *consts)


def _layer(x, mem, rel_bias, norm1_g, w_in, mem_norm_g, w_mem_kv, sink_logit, w_branch_a,
           w_branch_b, w_branch_m, w_out, norm2_g, w_up, w_down, final_norm_g, final_norm):
    b, seq, _ = x.shape
    n_mem = mem.shape[1]
    t = b * seq
    x2d = x.reshape(t, D_MODEL)

    off = np.cumsum((0,) + IN_SPLITS)
    half_heads = N_HEADS_A // 2
    scale = HEAD_DIM ** -0.5 * LOG2E
    w_qa = (w_in[:, off[0]:off[1]].reshape(D_MODEL, 2, half_heads, HEAD_DIM)
            .transpose(0, 2, 1, 3).reshape(D_MODEL, W_A)) * scale
    parts = [w_qa, w_in[:, off[1]:off[3]]]
    for g in range(len(DILATED_GROUPS)):
        lo, hi = g * W_B_OUT, (g + 1) * W_B_OUT
        parts += [w_in[:, off[3] + lo:off[3] + hi] * scale, w_in[:, off[4] + lo:off[4] + hi],
                  w_in[:, off[5] + lo:off[5] + hi]]
    parts.append(w_in[:, off[6]:off[7]] * (HEAD_DIM_M ** -0.5 * LOG2E))
    w_qkv = jnp.concatenate(parts, axis=1).astype(BF16)
    plan = ((W_A, 1), (2 * W_KV_A, 1)) + tuple((3 * W_B_OUT, dil) for _, dil in DILATED_GROUPS) + ((W_M, 1),)
    qa, kva, qkv0, qkv1, qkv2, qm = _in_proj(x, norm1_g.reshape(1, D_MODEL), w_qkv, plan, tm=1024)

    bias_a = _bias_tiles(rel_bias, np.arange(N_HEADS_A), WIN_A, 1, 3 * WIN_A, (0, -WIN_A, -2 * WIN_A))
    oa = _attn_win(sink_logit.astype(F32) * LOG2E, qa, kva, bias_a, b, seq)

    obs, lbs = [], []
    for g, ((win, dil), qkv) in enumerate(zip(DILATED_GROUPS, (qkv0, qkv1, qkv2))):
        half = win // (2 * dil)
        length = seq // dil
        kw = min(4 * half, length)
        deltas = (0, -half, -2 * half) if length > kw else (0,)
        hcols = N_HEADS_A + g * HB_PER_GROUP + np.arange(HB_PER_GROUP)
        bias_g = _bias_tiles(rel_bias, hcols, half, dil, kw, deltas)
        o, l = _attn_dil(qkv.reshape(b, dil, length, 3 * W_B_OUT), bias_g, dil, half)
        obs.append(o)
        lbs.append(l)

    mkv, = _in_proj(mem.reshape(1, b * n_mem, D_MODEL), mem_norm_g.reshape(1, D_MODEL),
                    w_mem_kv.astype(BF16), ((2 * W_M, 1),), tm=512)
    om = _attn_mem(qm, mkv, b, seq, n_mem)

    wa = (w_branch_a.reshape(2, half_heads, HEAD_DIM, D_MODEL).transpose(1, 0, 2, 3)
          .reshape(W_A, D_MODEL))
    y = _tail(x2d, oa, om, obs, lbs, norm1_g.reshape(1, D_MODEL), w_in[:, off[7]:].astype(BF16),
              wa.astype(BF16), w_branch_b.astype(BF16), w_branch_m.astype(BF16), w_out.astype(BF16),
              norm2_g.reshape(1, D_MODEL), w_up.astype(BF16), w_down.astype(BF16),
              final_norm_g.reshape(1, D_MODEL), tm=512, final_norm=final_norm)
    return y.reshape(b, seq, D_MODEL)


def _trunk(x, mem, rel_bias, norm1_g, w_in, mem_norm_g, w_mem_kv, sink_logit, w_branch_a,
           w_branch_b, w_branch_m, w_out, norm2_g, w_up, w_down, final_norm_g):
    depth = w_in.shape[0]
    for layer in range(depth):
        x = _layer(x, mem, rel_bias, norm1_g[layer], w_in[layer], mem_norm_g[layer], w_mem_kv[layer],
                   sink_logit[layer], w_branch_a[layer], w_branch_b[layer], w_branch_m[layer],
                   w_out[layer], norm2_g[layer], w_up[layer], w_down[layer], final_norm_g,
                   final_norm=layer == depth - 1)
    return x


def kernel(x_prompt, x_sample, mem_prompt, mem_sample, rel_bias, norm1_g, w_in, mem_norm_g, w_mem_kv,
           sink_logit, w_branch_a, w_branch_b, w_branch_m, w_out, norm2_g, w_up, w_down, final_norm_g):
    weights = (rel_bias, norm1_g, w_in, mem_norm_g, w_mem_kv, sink_logit, w_branch_a, w_branch_b,
               w_branch_m, w_out, norm2_g, w_up, w_down, final_norm_g)
    return (_trunk(x_prompt, mem_prompt, *weights), _trunk(x_sample, mem_sample, *weights))
```

```python
import functools

import numpy as np
import jax
import jax.numpy as jnp
from jax import lax
from jax.experimental import pallas as pl
from jax.experimental.pallas import tpu as pltpu

D_MODEL = 1024
HEAD_DIM = 64
N_HEADS_A = 8
N_KV_A = 2
WIN_A = 128
DILATED_GROUPS = ((128, 1), (512, 4), (2048, 16))
HB_PER_GROUP = 4
N_HEADS_B = HB_PER_GROUP * len(DILATED_GROUPS)
N_HEADS_M = 4
HEAD_DIM_M = 128
D_FF = 4 * D_MODEL
NUM_BUCKETS = 32
MAX_DIST = 1024
EPS = 1e-6
NEG = -1e30

W_A = N_HEADS_A * HEAD_DIM
W_KV_A = N_KV_A * HEAD_DIM
W_B = N_HEADS_B * HEAD_DIM
W_B_OUT = HB_PER_GROUP * HEAD_DIM
W_M = N_HEADS_M * HEAD_DIM_M
IN_SPLITS = (W_A, W_KV_A, W_KV_A, W_B, W_B, W_B, W_M, D_MODEL, D_MODEL, D_MODEL)

LANES = 128
QBLK = 128
LOOKAHEAD = 2
LOOKAHEAD_DIL = 3
LOG2E = 1.4426950408889634
FF_CHUNK = 2048
TAIL_CHAINS = 2
VMEM_LIMIT = 60 * 1024 * 1024
BF16 = jnp.bfloat16
F32 = jnp.float32


def _t5_bucket(rel):
    half = NUM_BUCKETS // 2
    ret = (rel > 0).astype(np.int32) * half
    n = np.abs(rel)
    max_exact = half // 2
    large = max_exact + (np.log(np.maximum(n, 1) / max_exact) / np.log(MAX_DIST / max_exact)
                         * (half - max_exact)).astype(np.int32)
    large = np.minimum(large, half - 1)
    return (ret + np.where(n < max_exact, n, large)).astype(np.int32)


def _bias_tiles(rel_bias, cols, half, dil, kw, deltas):
    period = QBLK + kw
    k = np.arange(period)
    off = np.where(k < kw, k, k - period)[None, :] + np.asarray(deltas)[:, None]
    valid = np.abs(off) <= half
    bucket = _t5_bucket(dil * np.clip(off, -half, half))
    v = jnp.transpose(rel_bias[bucket][:, :, cols].astype(F32), (0, 2, 1)) * LOG2E
    v = jnp.where(valid[:, None, :], v, NEG)
    t = jnp.tile(v, (1, 1, QBLK))[:, :, :QBLK * (period - 1)]
    return t.reshape(len(deltas), len(cols), QBLK, period - 1)[:, :, :, :kw]


def _params(sem, flags=None):
    return pltpu.CompilerParams(dimension_semantics=sem, vmem_limit_bytes=VMEM_LIMIT, flags=flags)


def _rms(x, g):
    return x * lax.rsqrt(jnp.mean(x * x, axis=-1, keepdims=True) + EPS) * g


def _in_proj_kernel(x_ref, g_ref, w_ref, *refs, plan, tm, chains):
    o_refs = refs[:len(plan)]
    hs_ref = refs[len(plan)] if len(refs) > len(plan) else None
    half = tm // chains

    def chain(j):
        base = j * half
        h = _rms(x_ref[base:base + half], g_ref[...])
        hb = h.astype(BF16)
        if hs_ref is not None:
            for k in range(D_MODEL // LANES):
                hs_ref[k, base:base + half] = h[:, k * LANES:(k + 1) * LANES]
        c = 0
        for o_ref, (wd, dil) in zip(o_refs, plan):
            rows = half // dil
            if dil == 1:
                lhs = hb
            else:
                lhs = jnp.concatenate(
                    [jnp.concatenate([hs_ref[k, pl.ds(base + r, rows, stride=dil), :]
                                      for k in range(D_MODEL // LANES)], axis=1)
                     for r in range(dil)], axis=0).astype(BF16)
            for s in range(0, wd, 512):
                e = min(wd, s + 512)
                res = jnp.dot(lhs, w_ref[:, c + s:c + e], preferred_element_type=F32).astype(o_ref.dtype)
                yield
                if dil == 1:
                    o_ref[base:base + half, s:e] = res
                else:
                    for r in range(dil):
                        o_ref[r, j * rows:(j + 1) * rows, s:e] = res[r * rows:(r + 1) * rows]
            c += wd

    done = object()
    live = [chain(j) for j in range(chains)]
    while live:
        live = [g for g in live if next(g, done) is not done]


def _in_proj(x, g, w, plan, tm):
    b, seq, _ = x.shape
    nt = seq // tm
    out_specs, out_shape = [], []
    for wd, dil in plan:
        if dil == 1:
            out_specs.append(pl.BlockSpec((tm, wd), lambda i, j: (i * nt + j, 0)))
            out_shape.append(jax.ShapeDtypeStruct((b * seq, wd), BF16))
        else:
            out_specs.append(pl.BlockSpec((None, dil, tm // dil, wd), lambda i, j: (i, 0, j, 0)))
            out_shape.append(jax.ShapeDtypeStruct((b, dil, seq // dil, wd), BF16))
    dilated = any(dil > 1 for _, dil in plan)
    return pl.pallas_call(
        functools.partial(_in_proj_kernel, plan=plan, tm=tm, chains=tm // 512),
        grid=(b, nt),
        in_specs=[pl.BlockSpec((None, tm, D_MODEL), lambda i, j: (i, j, 0)),
                  pl.BlockSpec((1, D_MODEL), lambda i, j: (0, 0)),
                  pl.BlockSpec(w.shape, lambda i, j: (0, 0), pipeline_mode=pl.Buffered(1))],
        out_specs=out_specs,
        out_shape=out_shape,
        scratch_shapes=[pltpu.VMEM((D_MODEL // LANES, tm, LANES), F32)] if dilated else [],
        compiler_params=_params(("parallel", "arbitrary")),
        name="in_proj",
    )(x, g, w)


def _pair_stream(units, emit, lookahead):
    cache = {}

    def scores(u):
        cache[u] = units[u]()
        return cache[u][0]()

    def parts(s, sink):
        m = jnp.max(s, axis=-1, keepdims=True)
        if sink is not None:
            m = jnp.maximum(m, sink)
        p = jnp.exp2(s - m)
        return m, p.astype(BF16), jnp.sum(p, axis=-1, keepdims=True)

    pending = [scores(u) for u in range(min(lookahead, len(units)))]
    for u in range(len(units)):
        s_lo, s_hi = pending.pop(0)
        if u + lookahead < len(units):
            pending.append(scores(u + lookahead))
        _, value_fn, sinks = cache.pop(u)
        m_lo, p_lo, d_lo = parts(s_lo, sinks and sinks[0])
        m_hi, p_hi, d_hi = parts(s_hi, sinks and sinks[1])
        o = value_fn(p_lo, p_hi)
        low = lax.broadcasted_iota(jnp.int32, o.shape, 1) < HEAD_DIM
        m = jnp.where(low, m_lo, m_hi)
        den = jnp.where(low, d_lo, d_hi)
        if sinks:
            den = den + jnp.exp2(jnp.where(low, sinks[0], sinks[1]) - m)
        emit(u, o / den, m + jnp.log2(den))


def _half(x2, hi):
    lane = lax.broadcasted_iota(jnp.int32, x2.shape, 1)
    keep = lane >= HEAD_DIM if hi else lane < HEAD_DIM
    return jnp.where(keep, x2, jnp.zeros_like(x2))


def _window(n, nblk, half, length, kw):
    if nblk == 1:
        return 0, 0
    return min(max(n * QBLK - half, 0), length - kw), (0 if n == 0 else 2 if n == nblk - 1 else 1)


_NT = (((1,), (1,)), ((), ()))


def _attn_win_kernel(sink_ref, q_ref, kv_ref, bias_ref, o_ref, *, seq):
    nblk = seq // QBLK
    kw = 3 * WIN_A
    half_heads = N_HEADS_A // 2
    stacks = {}

    def stacked(n, ws):
        if n not in stacks:
            k2 = kv_ref[ws:ws + kw, 0:LANES]
            v2 = kv_ref[ws:ws + kw, LANES:2 * LANES]
            stacks.clear()
            stacks[n] = (jnp.concatenate([_half(k2, False), _half(k2, True)], axis=0),
                         jnp.concatenate([_half(v2, False), _half(v2, True)], axis=0))
        return stacks[n]

    units, where = [], []
    for n in range(nblk):
        ws, var = _window(n, nblk, WIN_A, seq, kw)
        for t in range(half_heads):

            def unit(n=n, t=t, ws=ws, var=var):
                km, vm = stacked(n, ws)

                def score_fn():
                    s = lax.dot_general(q_ref[n * QBLK:(n + 1) * QBLK, t * LANES:(t + 1) * LANES], km,
                                        _NT, preferred_element_type=F32)
                    return s[:, :kw] + bias_ref[var, t], s[:, kw:] + bias_ref[var, half_heads + t]

                def value_fn(p_lo, p_hi):
                    return jnp.dot(jnp.concatenate([p_lo, p_hi], axis=1), vm, preferred_element_type=F32)

                return score_fn, value_fn, (sink_ref[t], sink_ref[half_heads + t])

            units.append(unit)
            where.append((n, t))

    def emit(u, o, _):
        n, t = where[u]
        o_ref[n * QBLK:(n + 1) * QBLK, t * LANES:(t + 1) * LANES] = o.astype(o_ref.dtype)

    _pair_stream(units, emit, LOOKAHEAD)


def _attn_win(sink, qa, kva, bias, b, seq):
    return pl.pallas_call(
        functools.partial(_attn_win_kernel, seq=seq),
        grid=(b,),
        in_specs=[pl.BlockSpec(memory_space=pltpu.SMEM),
                  pl.BlockSpec((seq, W_A), lambda i: (i, 0)),
                  pl.BlockSpec((seq, 2 * LANES), lambda i: (i, 0)),
                  pl.BlockSpec(bias.shape, lambda i: (0, 0, 0, 0))],
        out_specs=pl.BlockSpec((seq, W_A), lambda i: (i, 0)),
        out_shape=jax.ShapeDtypeStruct((b * seq, W_A), BF16),
        compiler_params=_params(("parallel",)),
        name="attn_win",
    )(sink, qa, kva, bias)


def _attn_dil_kernel(qkv_ref, bias_ref, ol_ref, *, dil, length, half, kw):
    nblk = length // QBLK
    units, where = [], []
    for r in range(dil):
        for n in range(nblk):
            ws, var = _window(n, nblk, half, length, kw)
            for t in range(HB_PER_GROUP // 2):
                c = t * LANES

                def unit(r=r, n=n, c=c, t=t, ws=ws, var=var):
                    def score_fn():
                        q2 = qkv_ref[r, n * QBLK:(n + 1) * QBLK, c:c + LANES]
                        k2 = qkv_ref[r, ws:ws + kw, W_B_OUT + c:W_B_OUT + c + LANES]
                        return tuple(lax.dot_general(_half(q2, hi), k2, _NT, preferred_element_type=F32)
                                     + bias_ref[var, 2 * t + hi] for hi in (0, 1))

                    def value_fn(p_lo, p_hi):
                        v2 = qkv_ref[r, ws:ws + kw, 2 * W_B_OUT + c:2 * W_B_OUT + c + LANES]
                        o_lo = jnp.dot(p_lo, v2, preferred_element_type=F32)
                        o_hi = jnp.dot(p_hi, v2, preferred_element_type=F32)
                        return jnp.where(lax.broadcasted_iota(jnp.int32, o_lo.shape, 1) < HEAD_DIM, o_lo, o_hi)

                    return score_fn, value_fn, None

                units.append(unit)
                where.append((r, n, t))

    def emit(u, o, lse2):
        r, n, t = where[u]
        ol_ref[0, t, r, n * QBLK:(n + 1) * QBLK, :] = o
        ol_ref[1, t, r, n * QBLK:(n + 1) * QBLK, :] = lse2

    _pair_stream(units, emit, LOOKAHEAD_DIL if kw > QBLK else LOOKAHEAD)


def _attn_dil(qkv, bias, dil, half):
    b, _, length, _ = qkv.shape
    kw = bias.shape[-1]
    npair = HB_PER_GROUP // 2
    blk = (None, 2, npair, dil, length, LANES)
    return pl.pallas_call(
        functools.partial(_attn_dil_kernel, dil=dil, length=length, half=half, kw=kw),
        grid=(b,),
        in_specs=[pl.BlockSpec((None, dil, length, 3 * W_B_OUT), lambda i: (i, 0, 0, 0)),
                  pl.BlockSpec(bias.shape, lambda i: (0, 0, 0, 0))],
        out_specs=pl.BlockSpec(blk, lambda i: (i, 0, 0, 0, 0, 0)),
        out_shape=jax.ShapeDtypeStruct((b, 2, npair, dil, length, LANES), F32),
        compiler_params=_params(("parallel",)),
        name=f"attn_dil{dil}",
    )(qkv, bias)


def _attn_mem_kernel(q_ref, mkv_ref, o_ref, *, seq):
    units = [(r, h * HEAD_DIM_M) for r in range(0, seq, QBLK) for h in range(N_HEADS_M)]

    def scores(u):
        r, c = units[u]
        return lax.dot_general(q_ref[r:r + QBLK, c:c + HEAD_DIM_M], mkv_ref[:, c:c + HEAD_DIM_M], _NT,
                               preferred_element_type=F32)

    pending = [scores(u) for u in range(LOOKAHEAD)]
    for u, (r, c) in enumerate(units):
        s = pending.pop(0)
        if u + LOOKAHEAD < len(units):
            pending.append(scores(u + LOOKAHEAD))
        p = jnp.exp2(s - jnp.max(s, axis=-1, keepdims=True))
        den = jnp.sum(p, axis=-1, keepdims=True)
        o = jnp.dot(p.astype(BF16), mkv_ref[:, W_M + c:W_M + c + HEAD_DIM_M], preferred_element_type=F32)
        o_ref[r:r + QBLK, c:c + HEAD_DIM_M] = (o / den).astype(o_ref.dtype)


def _attn_mem(qm, mkv, b, seq, n_mem):
    return pl.pallas_call(
        functools.partial(_attn_mem_kernel, seq=seq),
        grid=(b,),
        in_specs=[pl.BlockSpec((seq, W_M), lambda i: (i, 0)),
                  pl.BlockSpec((None, n_mem, 2 * W_M), lambda i: (i, 0, 0))],
        out_specs=pl.BlockSpec((seq, W_M), lambda i: (i, 0)),
        out_shape=jax.ShapeDtypeStruct((b * seq, W_M), BF16),
        compiler_params=_params(("parallel",)),
        name="attn_mem",
    )(qm, mkv.reshape(b, n_mem, 2 * W_M))


def _tail_kernel(x_ref, oa_ref, om_ref, ol0_ref, ol1_ref, ol2_ref,
                 g1_ref, wg_ref, wa_ref, wb_ref, wm_ref, wo_ref, g2_ref, wu_ref, wd_ref, gf_ref,
                 y_ref, mix_ref, *, final_norm):
    tm = x_ref.shape[0]
    half = tm // TAIL_CHAINS

    def dot(a, b):
        return jnp.dot(a, b, preferred_element_type=F32)

    def chain(j):
        rows = slice(j * half, (j + 1) * half)

        def slabs(ref, a, k):
            _, npair, dil, per, _ = ref.shape
            per //= TAIL_CHAINS
            if dil == 1:
                return jnp.concatenate([ref[a, p, 0, rows] for p in range(npair)], axis=1)
            for p in range(npair):
                for r in range(dil):
                    mix_ref[k, p, pl.ds(j * half + r, per, stride=dil), :] = ref[a, p, r, j * per:(j + 1) * per]
            return jnp.concatenate([mix_ref[k, p, rows] for p in range(npair)], axis=1)

        br_a = dot(oa_ref[rows], wa_ref[...])
        br_m = dot(om_ref[rows], wm_ref[...])
        yield
        x = x_ref[rows]
        h = _rms(x, g1_ref[...]).astype(BF16)
        g_a, g_b, g_m = (dot(h, wg_ref[:, k * D_MODEL:(k + 1) * D_MODEL]) for k in range(3))
        yield
        l0, l1, l2 = slabs(ol0_ref, 1, 0), slabs(ol1_ref, 1, 0), slabs(ol2_ref, 1, 1)
        mx = jnp.maximum(jnp.maximum(l0, l1), l2)
        e0, e1, e2 = jnp.exp2(l0 - mx), jnp.exp2(l1 - mx), jnp.exp2(l2 - mx)
        ob = (e0 * slabs(ol0_ref, 0, 0) + e1 * slabs(ol1_ref, 0, 2) + e2 * slabs(ol2_ref, 0, 3)) / (e0 + e1 + e2)
        br_b = dot(ob.astype(BF16), wb_ref[...])
        yield
        merged = jax.nn.sigmoid(g_a) * br_a + jax.nn.sigmoid(g_b) * br_b + jax.nn.sigmoid(g_m) * br_m
        x = x + dot(merged.astype(BF16), wo_ref[...])
        yield
        h2 = _rms(x, g2_ref[...]).astype(BF16)
        for c in range(D_FF // FF_CHUNK):
            u = dot(h2, wu_ref[:, c * FF_CHUNK:(c + 1) * FF_CHUNK])
            yield
            a = jnp.square(jnp.maximum(u, 0.0)).astype(BF16)
            x = x + dot(a, wd_ref[c * FF_CHUNK:(c + 1) * FF_CHUNK, :])
            yield
        if final_norm:
            x = _rms(x, gf_ref[...])
        y_ref[rows] = x

    done = object()
    chains = [chain(j) for j in range(TAIL_CHAINS)]
    while chains:
        chains = [g for g in chains if next(g, done) is not done]


def _tail(x2d, oa, om, ols, g1, wg, wa, wb, wm, wo, g2, wu, wd, gf, tm, final_norm):
    t = x2d.shape[0]
    npair = HB_PER_GROUP // 2
    nt = ols[0].shape[3] * ols[0].shape[4] // tm

    def rows(width):
        return pl.BlockSpec((tm, width), lambda i: (i, 0))

    def whole(a):
        return pl.BlockSpec(a.shape, lambda i: (0, 0), pipeline_mode=pl.Buffered(1))

    def slab(a):
        dil = a.shape[3]
        return pl.BlockSpec((None, 2, npair, dil, tm // dil, LANES), lambda i: (i // nt, 0, 0, 0, i % nt, 0))

    consts = (g1, wg, wa, wb, wm, wo, g2, wu, wd, gf)
    return pl.pallas_call(
        functools.partial(_tail_kernel, final_norm=final_norm),
        grid=(t // tm,),
        in_specs=([rows(D_MODEL), rows(W_A), rows(W_M)] + [slab(a) for a in ols]
                  + [whole(a) for a in consts]),
        out_specs=rows(D_MODEL),
        out_shape=jax.ShapeDtypeStruct((t, D_MODEL), F32),
        scratch_shapes=[pltpu.VMEM((4, npair, tm, LANES), F32)],
        compiler_params=_params(("parallel",)),
        name="tail",
    )(x2d, oa, om, *ols, *consts)


def _layer(x, mem, rel_bias, norm1_g, w_in, mem_norm_g, w_mem_kv, sink_logit, w_branch_a,
           w_branch_b, w_branch_m, w_out, norm2_g, w_up, w_down, final_norm_g, final_norm):
    b, seq, _ = x.shape
    n_mem = mem.shape[1]
    t = b * seq
    x2d = x.reshape(t, D_MODEL)

    off = np.cumsum((0,) + IN_SPLITS)
    half_heads = N_HEADS_A // 2
    scale = HEAD_DIM ** -0.5 * LOG2E
    w_qa = (w_in[:, off[0]:off[1]].reshape(D_MODEL, 2, half_heads, HEAD_DIM)
            .transpose(0, 2, 1, 3).reshape(D_MODEL, W_A)) * scale
    parts = [w_qa, w_in[:, off[1]:off[3]]]
    for g in range(len(DILATED_GROUPS)):
        lo, hi = g * W_B_OUT, (g + 1) * W_B_OUT
        parts += [w_in[:, off[3] + lo:off[3] + hi] * scale, w_in[:, off[4] + lo:off[4] + hi],
                  w_in[:, off[5] + lo:off[5] + hi]]
    parts.append(w_in[:, off[6]:off[7]] * (HEAD_DIM_M ** -0.5 * LOG2E))
    w_qkv = jnp.concatenate(parts, axis=1).astype(BF16)
    plan = ((W_A, 1), (2 * W_KV_A, 1)) + tuple((3 * W_B_OUT, dil) for _, dil in DILATED_GROUPS) + ((W_M, 1),)
    qa, kva, qkv0, qkv1, qkv2, qm = _in_proj(x, norm1_g.reshape(1, D_MODEL), w_qkv, plan, tm=1024)

    bias_a = _bias_tiles(rel_bias, np.arange(N_HEADS_A), WIN_A, 1, 3 * WIN_A, (0, -WIN_A, -2 * WIN_A))
    oa = _attn_win(sink_logit.astype(F32) * LOG2E, qa, kva, bias_a, b, seq)

    ols = []
    for g, ((win, dil), qkv) in enumerate(zip(DILATED_GROUPS, (qkv0, qkv1, qkv2))):
        half = win // (2 * dil)
        length = seq // dil
        kw = min(4 * half, length)
        deltas = (0, -half, -2 * half) if length > kw else (0,)
        hcols = N_HEADS_A + g * HB_PER_GROUP + np.arange(HB_PER_GROUP)
        bias_g = _bias_tiles(rel_bias, hcols, half, dil, kw, deltas)
        ols.append(_attn_dil(qkv.reshape(b, dil, length, 3 * W_B_OUT), bias_g, dil, half))

    mkv, = _in_proj(mem.reshape(1, b * n_mem, D_MODEL), mem_norm_g.reshape(1, D_MODEL),
                    w_mem_kv.astype(BF16), ((2 * W_M, 1),), tm=512)
    om = _attn_mem(qm, mkv, b, seq, n_mem)

    wa = (w_branch_a.reshape(2, half_heads, HEAD_DIM, D_MODEL).transpose(1, 0, 2, 3)
          .reshape(W_A, D_MODEL))
    y = _tail(x2d, oa, om, ols, norm1_g.reshape(1, D_MODEL), w_in[:, off[7]:].astype(BF16),
              wa.astype(BF16), w_branch_b.astype(BF16), w_branch_m.astype(BF16), w_out.astype(BF16),
              norm2_g.reshape(1, D_MODEL), w_up.astype(BF16), w_down.astype(BF16),
              final_norm_g.reshape(1, D_MODEL), tm=512, final_norm=final_norm)
    return y.reshape(b, seq, D_MODEL)


def _trunk(x, mem, rel_bias, norm1_g, w_in, mem_norm_g, w_mem_kv, sink_logit, w_branch_a,
           w_branch_b, w_branch_m, w_out, norm2_g, w_up, w_down, final_norm_g):
    depth = w_in.shape[0]
    for layer in range(depth):
        x = _layer(x, mem, rel_bias, norm1_g[layer], w_in[layer], mem_norm_g[layer], w_mem_kv[layer],
                   sink_logit[layer], w_branch_a[layer], w_branch_b[layer], w_branch_m[layer],
                   w_out[layer], norm2_g[layer], w_up[layer], w_down[layer], final_norm_g,
                   final_norm=layer == depth - 1)
    return x


def kernel(x_prompt, x_sample, mem_prompt, mem_sample, rel_bias, norm1_g, w_in, mem_norm_g, w_mem_kv,
           sink_logit, w_branch_a, w_branch_b, w_branch_m, w_out, norm2_g, w_up, w_down, final_norm_g):
    weights = (rel_bias, norm1_g, w_in, mem_norm_g, w_mem_kv, sink_logit, w_branch_a, w_branch_b,
               w_branch_m, w_out, norm2_g, w_up, w_down, final_norm_g)
    return (_trunk(x_prompt, mem_prompt, *weights), _trunk(x_sample, mem_sample, *weights))
```

```python
import functools

import numpy as np
import jax
import jax.numpy as jnp
from jax import lax
from jax.experimental import pallas as pl
from jax.experimental.pallas import tpu as pltpu

D_MODEL = 1024
HEAD_DIM = 64
N_HEADS_A = 8
N_KV_A = 2
WIN_A = 128
DILATED_GROUPS = ((128, 1), (512, 4), (2048, 16))
HB_PER_GROUP = 4
N_HEADS_B = HB_PER_GROUP * len(DILATED_GROUPS)
N_HEADS_M = 4
HEAD_DIM_M = 128
D_FF = 4 * D_MODEL
NUM_BUCKETS = 32
MAX_DIST = 1024
EPS = 1e-6
NEG = -1e30

W_A = N_HEADS_A * HEAD_DIM
W_KV_A = N_KV_A * HEAD_DIM
W_B = N_HEADS_B * HEAD_DIM
W_B_OUT = HB_PER_GROUP * HEAD_DIM
W_M = N_HEADS_M * HEAD_DIM_M
IN_SPLITS = (W_A, W_KV_A, W_KV_A, W_B, W_B, W_B, W_M, D_MODEL, D_MODEL, D_MODEL)

LANES = 128
QBLK = 128
LOOKAHEAD = 2
LOOKAHEAD_DIL = 3
LOG2E = 1.4426950408889634
FF_CHUNK = 2048
PROJ_CHUNK = 512
CHAIN_ROWS = 512
PROJ_ROWS = 2 * CHAIN_ROWS
TAIL_ROWS = 512
TAIL_CHAINS = 2
TAIL_SKEW = 3
VMEM_LIMIT = 60 * 1024 * 1024
BF16 = jnp.bfloat16
F32 = jnp.float32


def _t5_bucket(rel):
    half = NUM_BUCKETS // 2
    ret = (rel > 0).astype(np.int32) * half
    n = np.abs(rel)
    max_exact = half // 2
    large = max_exact + (np.log(np.maximum(n, 1) / max_exact) / np.log(MAX_DIST / max_exact)
                         * (half - max_exact)).astype(np.int32)
    large = np.minimum(large, half - 1)
    return (ret + np.where(n < max_exact, n, large)).astype(np.int32)


def _bias_tiles(rel_bias, cols, half, dil, kw, deltas):
    period = QBLK + kw
    k = np.arange(period)
    off = np.where(k < kw, k, k - period)[None, :] + np.asarray(deltas)[:, None]
    valid = np.abs(off) <= half
    bucket = _t5_bucket(dil * np.clip(off, -half, half))
    v = jnp.transpose(rel_bias[bucket][:, :, cols], (0, 2, 1)) * LOG2E
    v = jnp.where(valid[:, None, :], v, NEG)
    t = jnp.tile(v, (1, 1, QBLK))[:, :, :QBLK * (period - 1)]
    return t.reshape(len(deltas), len(cols), QBLK, period - 1)[:, :, :, :kw]


def _params(sem):
    return pltpu.CompilerParams(dimension_semantics=sem, vmem_limit_bytes=VMEM_LIMIT)


def _rms(x, g):
    return x * lax.rsqrt(jnp.mean(x * x, axis=-1, keepdims=True) + EPS) * g


def _in_proj_kernel(x_ref, g_ref, w_ref, *refs, plan, tm, chains):
    o_refs = refs[:len(plan)]
    hs_ref = refs[len(plan)] if len(refs) > len(plan) else None
    half = tm // chains

    def chain(j):
        base = j * half
        h = _rms(x_ref[base:base + half], g_ref[...])
        hb = h.astype(BF16)
        if hs_ref is not None:
            for k in range(D_MODEL // LANES):
                hs_ref[k, base:base + half] = h[:, k * LANES:(k + 1) * LANES]
        c = 0
        for o_ref, (wd, dil) in zip(o_refs, plan):
            rows = half // dil
            if dil == 1:
                lhs = hb
            else:
                lhs = jnp.concatenate(
                    [jnp.concatenate([hs_ref[k, pl.ds(base + r, rows, stride=dil), :]
                                      for k in range(D_MODEL // LANES)], axis=1)
                     for r in range(dil)], axis=0).astype(BF16)
            for s in range(0, wd, PROJ_CHUNK):
                e = min(wd, s + PROJ_CHUNK)
                res = jnp.dot(lhs, w_ref[:, c + s:c + e], preferred_element_type=F32).astype(o_ref.dtype)
                yield
                if dil == 1:
                    o_ref[base:base + half, s:e] = res
                else:
                    for r in range(dil):
                        o_ref[r, j * rows:(j + 1) * rows, s:e] = res[r * rows:(r + 1) * rows]
            c += wd

    done = object()
    live = [chain(j) for j in range(chains)]
    while live:
        live = [g for g in live if next(g, done) is not done]


def _in_proj(x, g, w, plan, tm):
    b, seq, _ = x.shape
    nt = seq // tm
    out_specs, out_shape = [], []
    for wd, dil in plan:
        if dil == 1:
            out_specs.append(pl.BlockSpec((tm, wd), lambda i, j: (i * nt + j, 0)))
            out_shape.append(jax.ShapeDtypeStruct((b * seq, wd), BF16))
        else:
            out_specs.append(pl.BlockSpec((None, dil, tm // dil, wd), lambda i, j: (i, 0, j, 0)))
            out_shape.append(jax.ShapeDtypeStruct((b, dil, seq // dil, wd), BF16))
    dilated = any(dil > 1 for _, dil in plan)
    return pl.pallas_call(
        functools.partial(_in_proj_kernel, plan=plan, tm=tm, chains=tm // CHAIN_ROWS),
        grid=(b, nt),
        in_specs=[pl.BlockSpec((None, tm, D_MODEL), lambda i, j: (i, j, 0)),
                  pl.BlockSpec((1, D_MODEL), lambda i, j: (0, 0)),
                  pl.BlockSpec(w.shape, lambda i, j: (0, 0), pipeline_mode=pl.Buffered(1))],
        out_specs=out_specs,
        out_shape=out_shape,
        scratch_shapes=[pltpu.VMEM((D_MODEL // LANES, tm, LANES), F32)] if dilated else [],
        compiler_params=_params(("parallel", "arbitrary")),
        name="in_proj",
    )(x, g, w)


def _pair_stream(units, emit, lookahead):
    cache = {}

    def scores(u):
        cache[u] = units[u]()
        return cache[u][0]()

    def parts(s, sink):
        m = jnp.max(s, axis=-1, keepdims=True)
        if sink is not None:
            m = jnp.maximum(m, sink)
        p = jnp.exp2(s - m)
        return m, p.astype(BF16), jnp.sum(p, axis=-1, keepdims=True)

    pending = [scores(u) for u in range(min(lookahead, len(units)))]
    for u in range(len(units)):
        s_lo, s_hi = pending.pop(0)
        if u + lookahead < len(units):
            pending.append(scores(u + lookahead))
        _, value_fn, sinks = cache.pop(u)
        m_lo, p_lo, d_lo = parts(s_lo, sinks and sinks[0])
        m_hi, p_hi, d_hi = parts(s_hi, sinks and sinks[1])
        o = value_fn(p_lo, p_hi)
        low = lax.broadcasted_iota(jnp.int32, o.shape, 1) < HEAD_DIM
        m = jnp.where(low, m_lo, m_hi)
        den = jnp.where(low, d_lo, d_hi)
        if sinks:
            den = den + jnp.exp2(jnp.where(low, sinks[0], sinks[1]) - m)
        emit(u, o / den, m + jnp.log2(den))


def _half(x2, hi):
    lane = lax.broadcasted_iota(jnp.int32, x2.shape, 1)
    keep = lane >= HEAD_DIM if hi else lane < HEAD_DIM
    return jnp.where(keep, x2, jnp.zeros_like(x2))


def _window(n, nblk, half, length, kw):
    if nblk == 1:
        return 0, 0
    return min(max(n * QBLK - half, 0), length - kw), (0 if n == 0 else 2 if n == nblk - 1 else 1)


_NT = (((1,), (1,)), ((), ()))


def _attn_win_kernel(sink_ref, q_ref, kv_ref, bias_ref, o_ref, *, seq):
    nblk = seq // QBLK
    kw = 3 * WIN_A
    half_heads = N_HEADS_A // 2
    stacks = {}

    def stacked(n, ws):
        if n not in stacks:
            k2 = kv_ref[ws:ws + kw, 0:LANES]
            v2 = kv_ref[ws:ws + kw, LANES:2 * LANES]
            stacks.clear()
            stacks[n] = (jnp.concatenate([_half(k2, False), _half(k2, True)], axis=0),
                         jnp.concatenate([_half(v2, False), _half(v2, True)], axis=0))
        return stacks[n]

    units, where = [], []
    for n in range(nblk):
        ws, var = _window(n, nblk, WIN_A, seq, kw)
        for t in range(half_heads):

            def unit(n=n, t=t, ws=ws, var=var):
                km, vm = stacked(n, ws)

                def score_fn():
                    s = lax.dot_general(q_ref[n * QBLK:(n + 1) * QBLK, t * LANES:(t + 1) * LANES], km,
                                        _NT, preferred_element_type=F32)
                    return s[:, :kw] + bias_ref[var, t], s[:, kw:] + bias_ref[var, half_heads + t]

                def value_fn(p_lo, p_hi):
                    return jnp.dot(jnp.concatenate([p_lo, p_hi], axis=1), vm, preferred_element_type=F32)

                return score_fn, value_fn, (sink_ref[t], sink_ref[half_heads + t])

            units.append(unit)
            where.append((n, t))

    def emit(u, o, _):
        n, t = where[u]
        o_ref[n * QBLK:(n + 1) * QBLK, t * LANES:(t + 1) * LANES] = o.astype(o_ref.dtype)

    _pair_stream(units, emit, LOOKAHEAD)


def _attn_win(sink, qa, kva, bias, b, seq):
    return pl.pallas_call(
        functools.partial(_attn_win_kernel, seq=seq),
        grid=(b,),
        in_specs=[pl.BlockSpec(memory_space=pltpu.SMEM),
                  pl.BlockSpec((seq, W_A), lambda i: (i, 0)),
                  pl.BlockSpec((seq, 2 * LANES), lambda i: (i, 0)),
                  pl.BlockSpec(bias.shape, lambda i: (0, 0, 0, 0))],
        out_specs=pl.BlockSpec((seq, W_A), lambda i: (i, 0)),
        out_shape=jax.ShapeDtypeStruct((b * seq, W_A), BF16),
        compiler_params=_params(("parallel",)),
        name="attn_win",
    )(sink, qa, kva, bias)


def _attn_dil_kernel(qkv_ref, bias_ref, ol_ref, *, dil, length, half, kw):
    nblk = length // QBLK
    units, where = [], []
    for r in range(dil):
        for n in range(nblk):
            ws, var = _window(n, nblk, half, length, kw)
            for t in range(HB_PER_GROUP // 2):
                c = t * LANES

                def unit(r=r, n=n, c=c, t=t, ws=ws, var=var):
                    def score_fn():
                        q2 = qkv_ref[r, n * QBLK:(n + 1) * QBLK, c:c + LANES]
                        k2 = qkv_ref[r, ws:ws + kw, W_B_OUT + c:W_B_OUT + c + LANES]
                        return tuple(lax.dot_general(_half(q2, hi), k2, _NT, preferred_element_type=F32)
                                     + bias_ref[var, 2 * t + hi] for hi in (0, 1))

                    def value_fn(p_lo, p_hi):
                        v2 = qkv_ref[r, ws:ws + kw, 2 * W_B_OUT + c:2 * W_B_OUT + c + LANES]
                        o_lo = jnp.dot(p_lo, v2, preferred_element_type=F32)
                        o_hi = jnp.dot(p_hi, v2, preferred_element_type=F32)
                        return jnp.where(lax.broadcasted_iota(jnp.int32, o_lo.shape, 1) < HEAD_DIM, o_lo, o_hi)

                    return score_fn, value_fn, None

                units.append(unit)
                where.append((r, n, t))

    def emit(u, o, lse2):
        r, n, t = where[u]
        ol_ref[0, t, r, n * QBLK:(n + 1) * QBLK, :] = o
        ol_ref[1, t, r, n * QBLK:(n + 1) * QBLK, :] = lse2

    _pair_stream(units, emit, LOOKAHEAD_DIL if kw > QBLK else LOOKAHEAD)


def _attn_dil(qkv, bias, dil, half):
    b, _, length, _ = qkv.shape
    kw = bias.shape[-1]
    npair = HB_PER_GROUP // 2
    blk = (None, 2, npair, dil, length, LANES)
    return pl.pallas_call(
        functools.partial(_attn_dil_kernel, dil=dil, length=length, half=half, kw=kw),
        grid=(b,),
        in_specs=[pl.BlockSpec((None, dil, length, 3 * W_B_OUT), lambda i: (i, 0, 0, 0)),
                  pl.BlockSpec(bias.shape, lambda i: (0, 0, 0, 0))],
        out_specs=pl.BlockSpec(blk, lambda i: (i, 0, 0, 0, 0, 0)),
        out_shape=jax.ShapeDtypeStruct((b, 2, npair, dil, length, LANES), F32),
        compiler_params=_params(("parallel",)),
        name=f"attn_dil{dil}",
    )(qkv, bias)


def _attn_mem_kernel(q_ref, mkv_ref, o_ref, *, seq):
    units = [(r, h * HEAD_DIM_M) for r in range(0, seq, QBLK) for h in range(N_HEADS_M)]

    def scores(u):
        r, c = units[u]
        return lax.dot_general(q_ref[r:r + QBLK, c:c + HEAD_DIM_M], mkv_ref[:, c:c + HEAD_DIM_M], _NT,
                               preferred_element_type=F32)

    pending = [scores(u) for u in range(LOOKAHEAD)]
    for u, (r, c) in enumerate(units):
        s = pending.pop(0)
        if u + LOOKAHEAD < len(units):
            pending.append(scores(u + LOOKAHEAD))
        p = jnp.exp2(s - jnp.max(s, axis=-1, keepdims=True))
        den = jnp.sum(p, axis=-1, keepdims=True)
        o = jnp.dot(p.astype(BF16), mkv_ref[:, W_M + c:W_M + c + HEAD_DIM_M], preferred_element_type=F32)
        o_ref[r:r + QBLK, c:c + HEAD_DIM_M] = (o / den).astype(o_ref.dtype)


def _attn_mem(qm, mkv, b, seq, n_mem):
    return pl.pallas_call(
        functools.partial(_attn_mem_kernel, seq=seq),
        grid=(b,),
        in_specs=[pl.BlockSpec((seq, W_M), lambda i: (i, 0)),
                  pl.BlockSpec((None, n_mem, 2 * W_M), lambda i: (i, 0, 0))],
        out_specs=pl.BlockSpec((seq, W_M), lambda i: (i, 0)),
        out_shape=jax.ShapeDtypeStruct((b * seq, W_M), BF16),
        compiler_params=_params(("parallel",)),
        name="attn_mem",
    )(qm, mkv.reshape(b, n_mem, 2 * W_M))


def _tail_kernel(x_ref, oa_ref, om_ref, ol0_ref, ol1_ref, ol2_ref,
                 g1_ref, wg_ref, wa_ref, wb_ref, wm_ref, wo_ref, g2_ref, wu_ref, wd_ref, gf_ref,
                 y_ref, mix_ref, *, final_norm):
    tm = x_ref.shape[0]
    half = tm // TAIL_CHAINS

    def dot(a, b):
        return jnp.dot(a, b, preferred_element_type=F32)

    def chain(j):
        rows = slice(j * half, (j + 1) * half)

        def slabs(ref, a, k):
            _, npair, dil, per, _ = ref.shape
            per //= TAIL_CHAINS
            if dil == 1:
                return jnp.concatenate([ref[a, p, 0, rows] for p in range(npair)], axis=1)
            for p in range(npair):
                for r in range(dil):
                    mix_ref[k, p, pl.ds(j * half + r, per, stride=dil), :] = ref[a, p, r, j * per:(j + 1) * per]
            return jnp.concatenate([mix_ref[k, p, rows] for p in range(npair)], axis=1)

        br_a = dot(oa_ref[rows], wa_ref[...])
        br_m = dot(om_ref[rows], wm_ref[...])
        yield
        x = x_ref[rows]
        h = _rms(x, g1_ref[...]).astype(BF16)
        g_a, g_b, g_m = (dot(h, wg_ref[:, k * D_MODEL:(k + 1) * D_MODEL]) for k in range(3))
        yield
        l0, l1, l2 = slabs(ol0_ref, 1, 0), slabs(ol1_ref, 1, 0), slabs(ol2_ref, 1, 1)
        mx = jnp.maximum(jnp.maximum(l0, l1), l2)
        e0, e1, e2 = jnp.exp2(l0 - mx), jnp.exp2(l1 - mx), jnp.exp2(l2 - mx)
        ob = (e0 * slabs(ol0_ref, 0, 0) + e1 * slabs(ol1_ref, 0, 2) + e2 * slabs(ol2_ref, 0, 3)) / (e0 + e1 + e2)
        br_b = dot(ob.astype(BF16), wb_ref[...])
        yield
        merged = jax.nn.sigmoid(g_a) * br_a + jax.nn.sigmoid(g_b) * br_b + jax.nn.sigmoid(g_m) * br_m
        x = x + dot(merged.astype(BF16), wo_ref[...])
        yield
        h2 = _rms(x, g2_ref[...]).astype(BF16)
        for c in range(D_FF // FF_CHUNK):
            u = dot(h2, wu_ref[:, c * FF_CHUNK:(c + 1) * FF_CHUNK])
            yield
            a = jnp.square(jnp.maximum(u, 0.0)).astype(BF16)
            x = x + dot(a, wd_ref[c * FF_CHUNK:(c + 1) * FF_CHUNK, :])
            yield
        if final_norm:
            x = _rms(x, gf_ref[...])
        y_ref[rows] = x

    done = object()
    waiting = [chain(j) for j in range(TAIL_CHAINS)]
    live, step = [], 0
    while live or waiting:
        if waiting and step % TAIL_SKEW == 0:
            live.append(waiting.pop(0))
        live = [g for g in live if next(g, done) is not done]
        step += 1


def _tail(x2d, oa, om, ols, g1, wg, wa, wb, wm, wo, g2, wu, wd, gf, tm, final_norm):
    t = x2d.shape[0]
    npair = HB_PER_GROUP // 2
    nt = ols[0].shape[3] * ols[0].shape[4] // tm

    def rows(width):
        return pl.BlockSpec((tm, width), lambda i: (i, 0))

    def whole(a):
        return pl.BlockSpec(a.shape, lambda i: (0, 0), pipeline_mode=pl.Buffered(1))

    def slab(a):
        dil = a.shape[3]
        return pl.BlockSpec((None, 2, npair, dil, tm // dil, LANES), lambda i: (i // nt, 0, 0, 0, i % nt, 0))

    consts = (g1, wg, wa, wb, wm, wo, g2, wu, wd, gf)
    return pl.pallas_call(
        functools.partial(_tail_kernel, final_norm=final_norm),
        grid=(t // tm,),
        in_specs=([rows(D_MODEL), rows(W_A), rows(W_M)] + [slab(a) for a in ols]
                  + [whole(a) for a in consts]),
        out_specs=rows(D_MODEL),
        out_shape=jax.ShapeDtypeStruct((t, D_MODEL), F32),
        scratch_shapes=[pltpu.VMEM((4, npair, tm, LANES), F32)],
        compiler_params=_params(("parallel",)),
        name="tail",
    )(x2d, oa, om, *ols, *consts)


def _layer(x, mem, rel_bias, norm1_g, w_in, mem_norm_g, w_mem_kv, sink_logit, w_branch_a,
           w_branch_b, w_branch_m, w_out, norm2_g, w_up, w_down, final_norm_g, final_norm):
    b, seq, _ = x.shape
    n_mem = mem.shape[1]
    t = b * seq
    x2d = x.reshape(t, D_MODEL)

    off = np.cumsum((0,) + IN_SPLITS)
    half_heads = N_HEADS_A // 2
    scale = HEAD_DIM ** -0.5 * LOG2E
    w_qa = (w_in[:, off[0]:off[1]].reshape(D_MODEL, 2, half_heads, HEAD_DIM)
            .transpose(0, 2, 1, 3).reshape(D_MODEL, W_A)) * scale
    parts = [w_qa, w_in[:, off[1]:off[3]]]
    for g in range(len(DILATED_GROUPS)):
        lo, hi = g * W_B_OUT, (g + 1) * W_B_OUT
        parts += [w_in[:, off[3] + lo:off[3] + hi] * scale, w_in[:, off[4] + lo:off[4] + hi],
                  w_in[:, off[5] + lo:off[5] + hi]]
    parts.append(w_in[:, off[6]:off[7]] * (HEAD_DIM_M ** -0.5 * LOG2E))
    w_qkv = jnp.concatenate(parts, axis=1).astype(BF16)
    plan = ((W_A, 1), (2 * W_KV_A, 1)) + tuple((3 * W_B_OUT, dil) for _, dil in DILATED_GROUPS) + ((W_M, 1),)
    qa, kva, qkv0, qkv1, qkv2, qm = _in_proj(x, norm1_g.reshape(1, D_MODEL), w_qkv, plan, tm=PROJ_ROWS)

    bias_a = _bias_tiles(rel_bias, np.arange(N_HEADS_A), WIN_A, 1, 3 * WIN_A, (0, -WIN_A, -2 * WIN_A))
    oa = _attn_win(sink_logit * LOG2E, qa, kva, bias_a, b, seq)

    ols = []
    for g, ((win, dil), qkv) in enumerate(zip(DILATED_GROUPS, (qkv0, qkv1, qkv2))):
        half = win // (2 * dil)
        length = seq // dil
        kw = min(4 * half, length)
        deltas = (0, -half, -2 * half) if length > kw else (0,)
        hcols = N_HEADS_A + g * HB_PER_GROUP + np.arange(HB_PER_GROUP)
        bias_g = _bias_tiles(rel_bias, hcols, half, dil, kw, deltas)
        ols.append(_attn_dil(qkv.reshape(b, dil, length, 3 * W_B_OUT), bias_g, dil, half))

    mkv, = _in_proj(mem.reshape(1, b * n_mem, D_MODEL), mem_norm_g.reshape(1, D_MODEL),
                    w_mem_kv.astype(BF16), ((2 * W_M, 1),), tm=CHAIN_ROWS)
    om = _attn_mem(qm, mkv, b, seq, n_mem)

    wa = (w_branch_a.reshape(2, half_heads, HEAD_DIM, D_MODEL).transpose(1, 0, 2, 3)
          .reshape(W_A, D_MODEL))
    y = _tail(x2d, oa, om, ols, norm1_g.reshape(1, D_MODEL), w_in[:, off[7]:].astype(BF16),
              wa.astype(BF16), w_branch_b.astype(BF16), w_branch_m.astype(BF16), w_out.astype(BF16),
              norm2_g.reshape(1, D_MODEL), w_up.astype(BF16), w_down.astype(BF16),
              final_norm_g.reshape(1, D_MODEL), tm=TAIL_ROWS, final_norm=final_norm)
    return y.reshape(b, seq, D_MODEL)


def _trunk(x, mem, rel_bias, norm1_g, w_in, mem_norm_g, w_mem_kv, sink_logit, w_branch_a,
           w_branch_b, w_branch_m, w_out, norm2_g, w_up, w_down, final_norm_g):
    depth = w_in.shape[0]
    for layer in range(depth):
        x = _layer(x, mem, rel_bias, norm1_g[layer], w_in[layer], mem_norm_g[layer], w_mem_kv[layer],
                   sink_logit[layer], w_branch_a[layer], w_branch_b[layer], w_branch_m[layer],
                   w_out[layer], norm2_g[layer], w_up[layer], w_down[layer], final_norm_g,
                   final_norm=layer == depth - 1)
    return x


def kernel(x_prompt, x_sample, mem_prompt, mem_sample, rel_bias, norm1_g, w_in, mem_norm_g, w_mem_kv,
           sink_logit, w_branch_a, w_branch_b, w_branch_m, w_out, norm2_g, w_up, w_down, final_norm_g):
    weights = (rel_bias, norm1_g, w_in, mem_norm_g, w_mem_kv, sink_logit, w_branch_a, w_branch_b,
               w_branch_m, w_out, norm2_g, w_up, w_down, final_norm_g)
    return (_trunk(x_prompt, mem_prompt, *weights), _trunk(x_sample, mem_sample, *weights))
```

```python
import functools

import numpy as np
import jax
import jax.numpy as jnp
from jax import lax
from jax.experimental import pallas as pl
from jax.experimental.pallas import tpu as pltpu

D_MODEL = 1024
HEAD_DIM = 64
N_HEADS_A = 8
N_KV_A = 2
WIN_A = 128
DILATED_GROUPS = ((128, 1), (512, 4), (2048, 16))
HB_PER_GROUP = 4
N_HEADS_B = HB_PER_GROUP * len(DILATED_GROUPS)
N_HEADS_M = 4
HEAD_DIM_M = 128
D_FF = 4 * D_MODEL
NUM_BUCKETS = 32
MAX_DIST = 1024
EPS = 1e-6
NEG = -1e30

W_A = N_HEADS_A * HEAD_DIM
W_KV_A = N_KV_A * HEAD_DIM
W_B = N_HEADS_B * HEAD_DIM
W_B_OUT = HB_PER_GROUP * HEAD_DIM
W_M = N_HEADS_M * HEAD_DIM_M
IN_SPLITS = (W_A, W_KV_A, W_KV_A, W_B, W_B, W_B, W_M, D_MODEL, D_MODEL, D_MODEL)

LANES = 128
QBLK = 128
LOOKAHEAD = 2
LOOKAHEAD_DIL = 3
LOG2E = 1.4426950408889634
FF_CHUNK = 2048
PROJ_CHUNK = 512
CHAIN_ROWS = 512
PROJ_ROWS = 2 * CHAIN_ROWS
TAIL_ROWS = 512
TAIL_CHAINS = 2
TAIL_SKEW = 3
VMEM_LIMIT = 60 * 1024 * 1024
BF16 = jnp.bfloat16
F32 = jnp.float32


def _t5_bucket(rel):
    half = NUM_BUCKETS // 2
    ret = (rel > 0).astype(np.int32) * half
    n = np.abs(rel)
    max_exact = half // 2
    large = max_exact + (np.log(np.maximum(n, 1) / max_exact) / np.log(MAX_DIST / max_exact)
                         * (half - max_exact)).astype(np.int32)
    large = np.minimum(large, half - 1)
    return (ret + np.where(n < max_exact, n, large)).astype(np.int32)


def _bias_tiles(rel_bias, cols, half, dil, kw, deltas):
    period = QBLK + kw
    k = np.arange(period)
    off = np.where(k < kw, k, k - period)[None, :] + np.asarray(deltas)[:, None]
    valid = np.abs(off) <= half
    bucket = _t5_bucket(dil * np.clip(off, -half, half))
    v = jnp.transpose(rel_bias[bucket][:, :, cols], (0, 2, 1)) * LOG2E
    v = jnp.where(valid[:, None, :], v, NEG)
    t = jnp.tile(v, (1, 1, QBLK))[:, :, :QBLK * (period - 1)]
    return t.reshape(len(deltas), len(cols), QBLK, period - 1)[:, :, :, :kw]


def _params(sem):
    return pltpu.CompilerParams(dimension_semantics=sem, vmem_limit_bytes=VMEM_LIMIT)


def _rms(x, g):
    return x * lax.rsqrt(jnp.mean(x * x, axis=-1, keepdims=True) + EPS) * g


def _in_proj_kernel(x_ref, g_ref, w_ref, *refs, plan, tm, chains):
    o_refs = refs[:len(plan)]
    hs_ref = refs[len(plan)] if len(refs) > len(plan) else None
    half = tm // chains

    def chain(j):
        base = j * half
        h = _rms(x_ref[base:base + half], g_ref[...])
        hb = h.astype(BF16)
        if hs_ref is not None:
            for k in range(D_MODEL // LANES):
                hs_ref[k, base:base + half] = h[:, k * LANES:(k + 1) * LANES]
        c = 0
        for o_ref, (wd, dil) in zip(o_refs, plan):
            rows = half // dil
            if dil == 1:
                lhs = hb
            else:
                lhs = jnp.concatenate(
                    [jnp.concatenate([hs_ref[k, pl.ds(base + r, rows, stride=dil), :]
                                      for k in range(D_MODEL // LANES)], axis=1)
                     for r in range(dil)], axis=0).astype(BF16)
            for s in range(0, wd, PROJ_CHUNK):
                e = min(wd, s + PROJ_CHUNK)
                res = jnp.dot(lhs, w_ref[:, c + s:c + e], preferred_element_type=F32).astype(o_ref.dtype)
                yield
                if dil == 1:
                    o_ref[base:base + half, s:e] = res
                else:
                    for r in range(dil):
                        o_ref[r, j * rows:(j + 1) * rows, s:e] = res[r * rows:(r + 1) * rows]
            c += wd

    done = object()
    live = [chain(j) for j in range(chains)]
    while live:
        live = [g for g in live if next(g, done) is not done]


def _in_proj(x, g, w, plan, tm):
    b, seq, _ = x.shape
    nt = seq // tm
    out_specs, out_shape = [], []
    for wd, dil in plan:
        if dil == 1:
            out_specs.append(pl.BlockSpec((tm, wd), lambda i, j: (i * nt + j, 0)))
            out_shape.append(jax.ShapeDtypeStruct((b * seq, wd), BF16))
        else:
            out_specs.append(pl.BlockSpec((None, dil, tm // dil, wd), lambda i, j: (i, 0, j, 0)))
            out_shape.append(jax.ShapeDtypeStruct((b, dil, seq // dil, wd), BF16))
    dilated = any(dil > 1 for _, dil in plan)
    return pl.pallas_call(
        functools.partial(_in_proj_kernel, plan=plan, tm=tm, chains=tm // CHAIN_ROWS),
        grid=(b, nt),
        in_specs=[pl.BlockSpec((None, tm, D_MODEL), lambda i, j: (i, j, 0)),
                  pl.BlockSpec((1, D_MODEL), lambda i, j: (0, 0)),
                  pl.BlockSpec(w.shape, lambda i, j: (0, 0), pipeline_mode=pl.Buffered(1))],
        out_specs=out_specs,
        out_shape=out_shape,
        scratch_shapes=[pltpu.VMEM((D_MODEL // LANES, tm, LANES), F32)] if dilated else [],
        compiler_params=_params(("parallel", "arbitrary")),
        name="in_proj",
    )(x, g, w)


def _pair_stream(units, emit, lookahead):
    cache = {}

    def scores(u):
        cache[u] = units[u]()
        return cache[u][0]()

    def parts(s, sink):
        m = jnp.max(s, axis=-1, keepdims=True)
        if sink is not None:
            m = jnp.maximum(m, sink)
        return m, jnp.exp2(s - m).astype(BF16)

    pending = [scores(u) for u in range(min(lookahead, len(units)))]
    for u in range(len(units)):
        s_lo, s_hi = pending.pop(0)
        if u + lookahead < len(units):
            pending.append(scores(u + lookahead))
        _, value_fn, sinks = cache.pop(u)
        m_lo, p_lo = parts(s_lo, sinks and sinks[0])
        m_hi, p_hi = parts(s_hi, sinks and sinks[1])
        o, den = value_fn(p_lo, p_hi)
        low = lax.broadcasted_iota(jnp.int32, o.shape, 1) < HEAD_DIM
        m = jnp.where(low, m_lo, m_hi)
        if sinks:
            den = den + jnp.exp2(jnp.where(low, sinks[0], sinks[1]) - m)
        emit(u, o / den, m + jnp.log2(den))


def _half(x2, hi):
    lane = lax.broadcasted_iota(jnp.int32, x2.shape, 1)
    keep = lane >= HEAD_DIM if hi else lane < HEAD_DIM
    return jnp.where(keep, x2, jnp.zeros_like(x2))


def _window(n, nblk, half, length, kw):
    if nblk == 1:
        return 0, 0
    return min(max(n * QBLK - half, 0), length - kw), (0 if n == 0 else 2 if n == nblk - 1 else 1)


_NT = (((1,), (1,)), ((), ()))


def _attn_win_kernel(sink_ref, q_ref, kv_ref, bias_ref, ones_ref, o_ref, *, seq):
    nblk = seq // QBLK
    kw = 3 * WIN_A
    half_heads = N_HEADS_A // 2
    stacks = {}

    def stacked(n, ws):
        if n not in stacks:
            k2 = kv_ref[ws:ws + kw, 0:LANES]
            v2 = kv_ref[ws:ws + kw, LANES:2 * LANES]
            stacks.clear()
            stacks[n] = (jnp.concatenate([_half(k2, False), _half(k2, True)], axis=0),
                         jnp.concatenate([jnp.concatenate([_half(v2, False), _half(v2, True)], axis=0),
                                          ones_ref[...]], axis=1))
        return stacks[n]

    units, where = [], []
    for n in range(nblk):
        ws, var = _window(n, nblk, WIN_A, seq, kw)
        for t in range(half_heads):

            def unit(n=n, t=t, ws=ws, var=var):
                km, vm = stacked(n, ws)

                def score_fn():
                    s = lax.dot_general(q_ref[n * QBLK:(n + 1) * QBLK, t * LANES:(t + 1) * LANES], km,
                                        _NT, preferred_element_type=F32)
                    return s[:, :kw] + bias_ref[var, t], s[:, kw:] + bias_ref[var, half_heads + t]

                def value_fn(p_lo, p_hi):
                    o = jnp.dot(jnp.concatenate([p_lo, p_hi], axis=1), vm, preferred_element_type=F32)
                    return o[:, :LANES], o[:, LANES:]

                return score_fn, value_fn, (sink_ref[t], sink_ref[half_heads + t])

            units.append(unit)
            where.append((n, t))

    def emit(u, o, _):
        n, t = where[u]
        o_ref[n * QBLK:(n + 1) * QBLK, t * LANES:(t + 1) * LANES] = o.astype(o_ref.dtype)

    _pair_stream(units, emit, LOOKAHEAD)


def _attn_win(sink, qa, kva, bias, b, seq):
    kw = bias.shape[-1]
    ones = np.zeros((2 * kw, LANES), np.float32)
    ones[:kw, :HEAD_DIM] = 1.0
    ones[kw:, HEAD_DIM:] = 1.0
    return pl.pallas_call(
        functools.partial(_attn_win_kernel, seq=seq),
        grid=(b,),
        in_specs=[pl.BlockSpec(memory_space=pltpu.SMEM),
                  pl.BlockSpec((seq, W_A), lambda i: (i, 0)),
                  pl.BlockSpec((seq, 2 * LANES), lambda i: (i, 0)),
                  pl.BlockSpec(bias.shape, lambda i: (0, 0, 0, 0)),
                  pl.BlockSpec(ones.shape, lambda i: (0, 0))],
        out_specs=pl.BlockSpec((seq, W_A), lambda i: (i, 0)),
        out_shape=jax.ShapeDtypeStruct((b * seq, W_A), BF16),
        compiler_params=_params(("parallel",)),
        name="attn_win",
    )(sink, qa, kva, bias, jnp.asarray(ones, BF16))


def _attn_dil_kernel(qkv_ref, bias_ref, ol_ref, *, dil, length, half, kw):
    nblk = length // QBLK
    units, where = [], []
    for r in range(dil):
        for n in range(nblk):
            ws, var = _window(n, nblk, half, length, kw)
            for t in range(HB_PER_GROUP // 2):
                c = t * LANES

                def unit(r=r, n=n, c=c, t=t, ws=ws, var=var):
                    def score_fn():
                        q2 = qkv_ref[r, n * QBLK:(n + 1) * QBLK, c:c + LANES]
                        k2 = qkv_ref[r, ws:ws + kw, W_B_OUT + c:W_B_OUT + c + LANES]
                        return tuple(lax.dot_general(_half(q2, hi), k2, _NT, preferred_element_type=F32)
                                     + bias_ref[var, 2 * t + hi] for hi in (0, 1))

                    def value_fn(p_lo, p_hi):
                        v2 = qkv_ref[r, ws:ws + kw, 2 * W_B_OUT + c:2 * W_B_OUT + c + LANES]
                        v2 = jnp.concatenate([v2, jnp.ones_like(v2)], axis=1)
                        o_lo = jnp.dot(p_lo, v2, preferred_element_type=F32)
                        o_hi = jnp.dot(p_hi, v2, preferred_element_type=F32)
                        low = lax.broadcasted_iota(jnp.int32, o_lo.shape, 1) % LANES < HEAD_DIM
                        o = jnp.where(low, o_lo, o_hi)
                        return o[:, :LANES], o[:, LANES:]

                    return score_fn, value_fn, None

                units.append(unit)
                where.append((r, n, t))

    def emit(u, o, lse2):
        r, n, t = where[u]
        ol_ref[0, t, r, n * QBLK:(n + 1) * QBLK, :] = o
        ol_ref[1, t, r, n * QBLK:(n + 1) * QBLK, :] = lse2

    _pair_stream(units, emit, LOOKAHEAD_DIL if kw > QBLK else LOOKAHEAD)


def _attn_dil(qkv, bias, dil, half):
    b, _, length, _ = qkv.shape
    kw = bias.shape[-1]
    npair = HB_PER_GROUP // 2
    blk = (None, 2, npair, dil, length, LANES)
    return pl.pallas_call(
        functools.partial(_attn_dil_kernel, dil=dil, length=length, half=half, kw=kw),
        grid=(b,),
        in_specs=[pl.BlockSpec((None, dil, length, 3 * W_B_OUT), lambda i: (i, 0, 0, 0)),
                  pl.BlockSpec(bias.shape, lambda i: (0, 0, 0, 0))],
        out_specs=pl.BlockSpec(blk, lambda i: (i, 0, 0, 0, 0, 0)),
        out_shape=jax.ShapeDtypeStruct((b, 2, npair, dil, length, LANES), F32),
        compiler_params=_params(("parallel",)),
        name=f"attn_dil{dil}",
    )(qkv, bias)


def _attn_mem_kernel(q_ref, mkv_ref, o_ref, *, seq):
    units = [(r, h * HEAD_DIM_M) for r in range(0, seq, QBLK) for h in range(N_HEADS_M)]

    def scores(u):
        r, c = units[u]
        return lax.dot_general(q_ref[r:r + QBLK, c:c + HEAD_DIM_M], mkv_ref[:, c:c + HEAD_DIM_M], _NT,
                               preferred_element_type=F32)

    pending = [scores(u) for u in range(LOOKAHEAD)]
    for u, (r, c) in enumerate(units):
        s = pending.pop(0)
        if u + LOOKAHEAD < len(units):
            pending.append(scores(u + LOOKAHEAD))
        p = jnp.exp2(s - jnp.max(s, axis=-1, keepdims=True)).astype(BF16)
        mv = mkv_ref[:, W_M + c:W_M + c + HEAD_DIM_M]
        o = jnp.dot(p, jnp.concatenate([mv, jnp.ones_like(mv)], axis=1), preferred_element_type=F32)
        o_ref[r:r + QBLK, c:c + HEAD_DIM_M] = (o[:, :HEAD_DIM_M] / o[:, HEAD_DIM_M:]).astype(o_ref.dtype)


def _attn_mem(qm, mkv, b, seq, n_mem):
    return pl.pallas_call(
        functools.partial(_attn_mem_kernel, seq=seq),
        grid=(b,),
        in_specs=[pl.BlockSpec((seq, W_M), lambda i: (i, 0)),
                  pl.BlockSpec((None, n_mem, 2 * W_M), lambda i: (i, 0, 0))],
        out_specs=pl.BlockSpec((seq, W_M), lambda i: (i, 0)),
        out_shape=jax.ShapeDtypeStruct((b * seq, W_M), BF16),
        compiler_params=_params(("parallel",)),
        name="attn_mem",
    )(qm, mkv.reshape(b, n_mem, 2 * W_M))


def _tail_kernel(x_ref, oa_ref, om_ref, ol0_ref, ol1_ref, ol2_ref,
                 g1_ref, wg_ref, wa_ref, wb_ref, wm_ref, wo_ref, g2_ref, wu_ref, wd_ref, gf_ref,
                 y_ref, mix_ref, *, final_norm):
    tm = x_ref.shape[0]
    half = tm // TAIL_CHAINS

    def dot(a, b):
        return jnp.dot(a, b, preferred_element_type=F32)

    def chain(j):
        rows = slice(j * half, (j + 1) * half)

        def slabs(ref, a, k):
            _, npair, dil, per, _ = ref.shape
            per //= TAIL_CHAINS
            if dil == 1:
                return jnp.concatenate([ref[a, p, 0, rows] for p in range(npair)], axis=1)
            for p in range(npair):
                for r in range(dil):
                    mix_ref[k, p, pl.ds(j * half + r, per, stride=dil), :] = ref[a, p, r, j * per:(j + 1) * per]
            return jnp.concatenate([mix_ref[k, p, rows] for p in range(npair)], axis=1)

        br_a = dot(oa_ref[rows], wa_ref[...])
        br_m = dot(om_ref[rows], wm_ref[...])
        yield
        x = x_ref[rows]
        h = _rms(x, g1_ref[...]).astype(BF16)
        g_a, g_b, g_m = (dot(h, wg_ref[:, k * D_MODEL:(k + 1) * D_MODEL]) for k in range(3))
        yield
        l0, l1, l2 = slabs(ol0_ref, 1, 0), slabs(ol1_ref, 1, 0), slabs(ol2_ref, 1, 1)
        mx = jnp.maximum(jnp.maximum(l0, l1), l2)
        e0, e1, e2 = jnp.exp2(l0 - mx), jnp.exp2(l1 - mx), jnp.exp2(l2 - mx)
        ob = (e0 * slabs(ol0_ref, 0, 0) + e1 * slabs(ol1_ref, 0, 2) + e2 * slabs(ol2_ref, 0, 3)) / (e0 + e1 + e2)
        br_b = dot(ob.astype(BF16), wb_ref[...])
        yield
        merged = jax.nn.sigmoid(g_a) * br_a + jax.nn.sigmoid(g_b) * br_b + jax.nn.sigmoid(g_m) * br_m
        x = x + dot(merged.astype(BF16), wo_ref[...])
        yield
        h2 = _rms(x, g2_ref[...]).astype(BF16)
        for c in range(D_FF // FF_CHUNK):
            u = dot(h2, wu_ref[:, c * FF_CHUNK:(c + 1) * FF_CHUNK])
            yield
            a = jnp.square(jnp.maximum(u, 0.0)).astype(BF16)
            x = x + dot(a, wd_ref[c * FF_CHUNK:(c + 1) * FF_CHUNK, :])
            yield
        if final_norm:
            x = _rms(x, gf_ref[...])
        y_ref[rows] = x

    done = object()
    waiting = [chain(j) for j in range(TAIL_CHAINS)]
    live, step = [], 0
    while live or waiting:
        if waiting and step % TAIL_SKEW == 0:
            live.append(waiting.pop(0))
        live = [g for g in live if next(g, done) is not done]
        step += 1


def _tail(x2d, oa, om, ols, g1, wg, wa, wb, wm, wo, g2, wu, wd, gf, tm, final_norm):
    t = x2d.shape[0]
    npair = HB_PER_GROUP // 2
    nt = ols[0].shape[3] * ols[0].shape[4] // tm

    def rows(width):
        return pl.BlockSpec((tm, width), lambda i: (i, 0))

    def whole(a):
        return pl.BlockSpec(a.shape, lambda i: (0, 0), pipeline_mode=pl.Buffered(1))

    def slab(a):
        dil = a.shape[3]
        return pl.BlockSpec((None, 2, npair, dil, tm // dil, LANES), lambda i: (i // nt, 0, 0, 0, i % nt, 0))

    consts = (g1, wg, wa, wb, wm, wo, g2, wu, wd, gf)
    return pl.pallas_call(
        functools.partial(_tail_kernel, final_norm=final_norm),
        grid=(t // tm,),
        in_specs=([rows(D_MODEL), rows(W_A), rows(W_M)] + [slab(a) for a in ols]
                  + [whole(a) for a in consts]),
        out_specs=rows(D_MODEL),
        out_shape=jax.ShapeDtypeStruct((t, D_MODEL), F32),
        scratch_shapes=[pltpu.VMEM((4, npair, tm, LANES), F32)],
        compiler_params=_params(("parallel",)),
        name="tail",
    )(x2d, oa, om, *ols, *consts)


def _layer(x, mem, rel_bias, norm1_g, w_in, mem_norm_g, w_mem_kv, sink_logit, w_branch_a,
           w_branch_b, w_branch_m, w_out, norm2_g, w_up, w_down, final_norm_g, final_norm):
    b, seq, _ = x.shape
    n_mem = mem.shape[1]
    t = b * seq
    x2d = x.reshape(t, D_MODEL)

    off = np.cumsum((0,) + IN_SPLITS)
    half_heads = N_HEADS_A // 2
    scale = HEAD_DIM ** -0.5 * LOG2E
    w_qa = (w_in[:, off[0]:off[1]].reshape(D_MODEL, 2, half_heads, HEAD_DIM)
            .transpose(0, 2, 1, 3).reshape(D_MODEL, W_A)) * scale
    parts = [w_qa, w_in[:, off[1]:off[3]]]
    for g in range(len(DILATED_GROUPS)):
        lo, hi = g * W_B_OUT, (g + 1) * W_B_OUT
        parts += [w_in[:, off[3] + lo:off[3] + hi] * scale, w_in[:, off[4] + lo:off[4] + hi],
                  w_in[:, off[5] + lo:off[5] + hi]]
    parts.append(w_in[:, off[6]:off[7]] * (HEAD_DIM_M ** -0.5 * LOG2E))
    w_qkv = jnp.concatenate(parts, axis=1).astype(BF16)
    plan = ((W_A, 1), (2 * W_KV_A, 1)) + tuple((3 * W_B_OUT, dil) for _, dil in DILATED_GROUPS) + ((W_M, 1),)
    qa, kva, qkv0, qkv1, qkv2, qm = _in_proj(x, norm1_g.reshape(1, D_MODEL), w_qkv, plan, tm=PROJ_ROWS)

    bias_a = _bias_tiles(rel_bias, np.arange(N_HEADS_A), WIN_A, 1, 3 * WIN_A, (0, -WIN_A, -2 * WIN_A))
    oa = _attn_win(sink_logit * LOG2E, qa, kva, bias_a, b, seq)

    ols = []
    for g, ((win, dil), qkv) in enumerate(zip(DILATED_GROUPS, (qkv0, qkv1, qkv2))):
        half = win // (2 * dil)
        length = seq // dil
        kw = min(4 * half, length)
        deltas = (0, -half, -2 * half) if length > kw else (0,)
        hcols = N_HEADS_A + g * HB_PER_GROUP + np.arange(HB_PER_GROUP)
        bias_g = _bias_tiles(rel_bias, hcols, half, dil, kw, deltas)
        ols.append(_attn_dil(qkv.reshape(b, dil, length, 3 * W_B_OUT), bias_g, dil, half))

    mkv, = _in_proj(mem.reshape(1, b * n_mem, D_MODEL), mem_norm_g.reshape(1, D_MODEL),
                    w_mem_kv.astype(BF16), ((2 * W_M, 1),), tm=CHAIN_ROWS)
    om = _attn_mem(qm, mkv, b, seq, n_mem)

    wa = (w_branch_a.reshape(2, half_heads, HEAD_DIM, D_MODEL).transpose(1, 0, 2, 3)
          .reshape(W_A, D_MODEL))
    y = _tail(x2d, oa, om, ols, norm1_g.reshape(1, D_MODEL), w_in[:, off[7]:].astype(BF16),
              wa.astype(BF16), w_branch_b.astype(BF16), w_branch_m.astype(BF16), w_out.astype(BF16),
              norm2_g.reshape(1, D_MODEL), w_up.astype(BF16), w_down.astype(BF16),
              final_norm_g.reshape(1, D_MODEL), tm=TAIL_ROWS, final_norm=final_norm)
    return y.reshape(b, seq, D_MODEL)


def _trunk(x, mem, rel_bias, norm1_g, w_in, mem_norm_g, w_mem_kv, sink_logit, w_branch_a,
           w_branch_b, w_branch_m, w_out, norm2_g, w_up, w_down, final_norm_g):
    depth = w_in.shape[0]
    for layer in range(depth):
        x = _layer(x, mem, rel_bias, norm1_g[layer], w_in[layer], mem_norm_g[layer], w_mem_kv[layer],
                   sink_logit[layer], w_branch_a[layer], w_branch_b[layer], w_branch_m[layer],
                   w_out[layer], norm2_g[layer], w_up[layer], w_down[layer], final_norm_g,
                   final_norm=layer == depth - 1)
    return x


def kernel(x_prompt, x_sample, mem_prompt, mem_sample, rel_bias, norm1_g, w_in, mem_norm_g, w_mem_kv,
           sink_logit, w_branch_a, w_branch_b, w_branch_m, w_out, norm2_g, w_up, w_down, final_norm_g):
    weights = (rel_bias, norm1_g, w_in, mem_norm_g, w_mem_kv, sink_logit, w_branch_a, w_branch_b,
               w_branch_m, w_out, norm2_g, w_up, w_down, final_norm_g)
    return (_trunk(x_prompt, mem_prompt, *weights), _trunk(x_sample, mem_sample, *weights))
```

```python
import functools

import numpy as np
import jax
import jax.numpy as jnp
from jax import lax
from jax.experimental import pallas as pl
from jax.experimental.pallas import tpu as pltpu

D_MODEL = 1024
HEAD_DIM = 64
N_HEADS_A = 8
N_KV_A = 2
WIN_A = 128
DILATED_GROUPS = ((128, 1), (512, 4), (2048, 16))
HB_PER_GROUP = 4
N_HEADS_B = HB_PER_GROUP * len(DILATED_GROUPS)
N_HEADS_M = 4
HEAD_DIM_M = 128
D_FF = 4 * D_MODEL
NUM_BUCKETS = 32
MAX_DIST = 1024
EPS = 1e-6
NEG = -1e30

W_A = N_HEADS_A * HEAD_DIM
W_KV_A = N_KV_A * HEAD_DIM
W_B = N_HEADS_B * HEAD_DIM
W_B_OUT = HB_PER_GROUP * HEAD_DIM
W_M = N_HEADS_M * HEAD_DIM_M
IN_SPLITS = (W_A, W_KV_A, W_KV_A, W_B, W_B, W_B, W_M, D_MODEL, D_MODEL, D_MODEL)

LANES = 128
QBLK = 128
LOOKAHEAD = 2
LOOKAHEAD_DIL = 5
LOOKAHEAD_DIL_NARROW = 6
LOG2E = 1.4426950408889634
FF_CHUNK = 2048
PROJ_CHUNK = 512
CHAIN_ROWS = 512
PROJ_ROWS = 2 * CHAIN_ROWS
TAIL_ROWS = 512
TAIL_CHAINS = 2
TAIL_SKEW = 3
VMEM_LIMIT = 60 * 1024 * 1024
BF16 = jnp.bfloat16
F32 = jnp.float32


def _t5_bucket(rel):
    half = NUM_BUCKETS // 2
    ret = (rel > 0).astype(np.int32) * half
    n = np.abs(rel)
    max_exact = half // 2
    large = max_exact + (np.log(np.maximum(n, 1) / max_exact) / np.log(MAX_DIST / max_exact)
                         * (half - max_exact)).astype(np.int32)
    large = np.minimum(large, half - 1)
    return (ret + np.where(n < max_exact, n, large)).astype(np.int32)


def _bias_tiles(rel_bias, cols, half, dil, kw, deltas):
    period = QBLK + kw
    k = np.arange(period)
    off = np.where(k < kw, k, k - period)[None, :] + np.asarray(deltas)[:, None]
    valid = np.abs(off) <= half
    bucket = _t5_bucket(dil * np.clip(off, -half, half))
    v = jnp.transpose(rel_bias[bucket][:, :, cols], (0, 2, 1))
    v = jnp.where(valid[:, None, :], v, NEG) * LOG2E
    t = jnp.tile(v, (1, 1, QBLK))[:, :, :QBLK * (period - 1)]
    return t.reshape(len(deltas), len(cols), QBLK, period - 1)[:, :, :, :kw]


def _params(sem):
    return pltpu.CompilerParams(dimension_semantics=sem, vmem_limit_bytes=VMEM_LIMIT)


def _rms(x, g):
    return x * lax.rsqrt(jnp.mean(x * x, axis=-1, keepdims=True) + EPS) * g


def _in_proj_kernel(x_ref, g_ref, w_ref, *refs, plan, tm, chains):
    o_refs = refs[:len(plan)]
    hs_ref = refs[len(plan)] if len(refs) > len(plan) else None
    half = tm // chains

    def chain(j):
        base = j * half
        h = _rms(x_ref[base:base + half], g_ref[...])
        hb = h.astype(BF16)
        if hs_ref is not None:
            for k in range(D_MODEL // LANES):
                hs_ref[k, base:base + half] = h[:, k * LANES:(k + 1) * LANES]
        c = 0
        for o_ref, (wd, dil) in zip(o_refs, plan):
            rows = half // dil
            if dil == 1:
                lhs = hb
            else:
                lhs = jnp.concatenate(
                    [jnp.concatenate([hs_ref[k, pl.ds(base + r, rows, stride=dil), :]
                                      for k in range(D_MODEL // LANES)], axis=1)
                     for r in range(dil)], axis=0).astype(BF16)
            for s in range(0, wd, PROJ_CHUNK):
                e = min(wd, s + PROJ_CHUNK)
                res = jnp.dot(lhs, w_ref[:, c + s:c + e], preferred_element_type=F32).astype(o_ref.dtype)
                yield
                if dil == 1:
                    o_ref[base:base + half, s:e] = res
                else:
                    for r in range(dil):
                        o_ref[r, j * rows:(j + 1) * rows, s:e] = res[r * rows:(r + 1) * rows]
            c += wd

    done = object()
    live = [chain(j) for j in range(chains)]
    while live:
        live = [g for g in live if next(g, done) is not done]


def _in_proj(x, g, w, plan, tm):
    b, seq, _ = x.shape
    nt = seq // tm
    out_specs, out_shape = [], []
    for wd, dil in plan:
        if dil == 1:
            out_specs.append(pl.BlockSpec((tm, wd), lambda i, j: (i * nt + j, 0)))
            out_shape.append(jax.ShapeDtypeStruct((b * seq, wd), BF16))
        else:
            out_specs.append(pl.BlockSpec((None, dil, tm // dil, wd), lambda i, j: (i, 0, j, 0)))
            out_shape.append(jax.ShapeDtypeStruct((b, dil, seq // dil, wd), BF16))
    dilated = any(dil > 1 for _, dil in plan)
    return pl.pallas_call(
        functools.partial(_in_proj_kernel, plan=plan, tm=tm, chains=tm // CHAIN_ROWS),
        grid=(b, nt),
        in_specs=[pl.BlockSpec((None, tm, D_MODEL), lambda i, j: (i, j, 0)),
                  pl.BlockSpec((1, D_MODEL), lambda i, j: (0, 0)),
                  pl.BlockSpec(w.shape, lambda i, j: (0, 0), pipeline_mode=pl.Buffered(1))],
        out_specs=out_specs,
        out_shape=out_shape,
        scratch_shapes=[pltpu.VMEM((D_MODEL // LANES, tm, LANES), F32)] if dilated else [],
        compiler_params=_params(("parallel", "arbitrary")),
        name="in_proj",
    )(x, g, w)


def _pair_stream(units, emit, lookahead):
    cache = {}

    def scores(u):
        cache[u] = units[u]()
        return cache[u][0]()

    def parts(s, sink):
        m = jnp.max(s, axis=-1, keepdims=True)
        if sink is not None:
            m = jnp.maximum(m, sink)
        return m, jnp.exp2(s - m).astype(BF16)

    pending = [scores(u) for u in range(min(lookahead, len(units)))]
    for u in range(len(units)):
        s_lo, s_hi = pending.pop(0)
        if u + lookahead < len(units):
            pending.append(scores(u + lookahead))
        _, value_fn, sinks = cache.pop(u)
        m_lo, p_lo = parts(s_lo, sinks and sinks[0])
        m_hi, p_hi = parts(s_hi, sinks and sinks[1])
        o, den = value_fn(p_lo, p_hi)
        low = lax.broadcasted_iota(jnp.int32, o.shape, 1) < HEAD_DIM
        m = jnp.where(low, m_lo, m_hi)
        if sinks:
            den = den + jnp.exp2(jnp.where(low, sinks[0], sinks[1]) - m)
        emit(u, o / den, m + jnp.log2(den))


def _half(x2, hi):
    lane = lax.broadcasted_iota(jnp.int32, x2.shape, 1)
    keep = lane >= HEAD_DIM if hi else lane < HEAD_DIM
    return jnp.where(keep, x2, jnp.zeros_like(x2))


def _window(n, nblk, half, length, kw):
    if nblk == 1:
        return 0, 0
    return min(max(n * QBLK - half, 0), length - kw), (0 if n == 0 else 2 if n == nblk - 1 else 1)


_NT = (((1,), (1,)), ((), ()))


def _attn_win_kernel(sink_ref, q_ref, kv_ref, bias_ref, ones_ref, o_ref, *, seq):
    nblk = seq // QBLK
    kw = 3 * WIN_A
    half_heads = N_HEADS_A // 2
    stacks = {}

    def stacked(n, ws):
        if n not in stacks:
            k2 = kv_ref[ws:ws + kw, 0:LANES]
            v2 = kv_ref[ws:ws + kw, LANES:2 * LANES]
            stacks.clear()
            stacks[n] = (jnp.concatenate([_half(k2, False), _half(k2, True)], axis=0),
                         jnp.concatenate([jnp.concatenate([_half(v2, False), _half(v2, True)], axis=0),
                                          ones_ref[...]], axis=1))
        return stacks[n]

    units, where = [], []
    for n in range(nblk):
        ws, var = _window(n, nblk, WIN_A, seq, kw)
        for t in range(half_heads):

            def unit(n=n, t=t, ws=ws, var=var):
                km, vm = stacked(n, ws)

                def score_fn():
                    s = lax.dot_general(q_ref[n * QBLK:(n + 1) * QBLK, t * LANES:(t + 1) * LANES], km,
                                        _NT, preferred_element_type=F32)
                    return s[:, :kw] + bias_ref[var, t], s[:, kw:] + bias_ref[var, half_heads + t]

                def value_fn(p_lo, p_hi):
                    o = jnp.dot(jnp.concatenate([p_lo, p_hi], axis=1), vm, preferred_element_type=F32)
                    return o[:, :LANES], o[:, LANES:]

                return score_fn, value_fn, (sink_ref[t], sink_ref[half_heads + t])

            units.append(unit)
            where.append((n, t))

    def emit(u, o, _):
        n, t = where[u]
        o_ref[n * QBLK:(n + 1) * QBLK, t * LANES:(t + 1) * LANES] = o.astype(o_ref.dtype)

    _pair_stream(units, emit, LOOKAHEAD)


def _attn_win(sink, qa, kva, bias, b, seq):
    kw = bias.shape[-1]
    ones = np.zeros((2 * kw, LANES), np.float32)
    ones[:kw, :HEAD_DIM] = 1.0
    ones[kw:, HEAD_DIM:] = 1.0
    return pl.pallas_call(
        functools.partial(_attn_win_kernel, seq=seq),
        grid=(b,),
        in_specs=[pl.BlockSpec(memory_space=pltpu.SMEM),
                  pl.BlockSpec((seq, W_A), lambda i: (i, 0)),
                  pl.BlockSpec((seq, 2 * LANES), lambda i: (i, 0)),
                  pl.BlockSpec(bias.shape, lambda i: (0, 0, 0, 0)),
                  pl.BlockSpec(ones.shape, lambda i: (0, 0))],
        out_specs=pl.BlockSpec((seq, W_A), lambda i: (i, 0)),
        out_shape=jax.ShapeDtypeStruct((b * seq, W_A), BF16),
        compiler_params=_params(("parallel",)),
        name="attn_win",
    )(sink, qa, kva, bias, jnp.asarray(ones, BF16))


def _attn_dil_kernel(qkv_ref, bias_ref, ol_ref, *, dil, length, half, kw):
    nblk = length // QBLK
    units, where = [], []
    for r in range(dil):
        for n in range(nblk):
            ws, var = _window(n, nblk, half, length, kw)
            for t in range(HB_PER_GROUP // 2):
                c = t * LANES

                def unit(r=r, n=n, c=c, t=t, ws=ws, var=var):
                    def score_fn():
                        q2 = qkv_ref[r, n * QBLK:(n + 1) * QBLK, c:c + LANES]
                        k2 = qkv_ref[r, ws:ws + kw, W_B_OUT + c:W_B_OUT + c + LANES]
                        return tuple(lax.dot_general(_half(q2, hi), k2, _NT, preferred_element_type=F32)
                                     + bias_ref[var, 2 * t + hi] for hi in (0, 1))

                    def value_fn(p_lo, p_hi):
                        v2 = qkv_ref[r, ws:ws + kw, 2 * W_B_OUT + c:2 * W_B_OUT + c + LANES]
                        v2 = jnp.concatenate([v2, jnp.ones_like(v2)], axis=1)
                        o_lo = jnp.dot(p_lo, v2, preferred_element_type=F32)
                        o_hi = jnp.dot(p_hi, v2, preferred_element_type=F32)
                        low = lax.broadcasted_iota(jnp.int32, o_lo.shape, 1) % LANES < HEAD_DIM
                        o = jnp.where(low, o_lo, o_hi)
                        return o[:, :LANES], o[:, LANES:]

                    return score_fn, value_fn, None

                units.append(unit)
                where.append((r, n, t))

    def emit(u, o, lse2):
        r, n, t = where[u]
        ol_ref[0, t, r, n * QBLK:(n + 1) * QBLK, :] = o
        ol_ref[1, t, r, n * QBLK:(n + 1) * QBLK, :] = lse2

    _pair_stream(units, emit, LOOKAHEAD_DIL if kw > QBLK else LOOKAHEAD_DIL_NARROW)


def _attn_dil(qkv, bias, dil, half):
    b, _, length, _ = qkv.shape
    kw = bias.shape[-1]
    npair = HB_PER_GROUP // 2
    blk = (None, 2, npair, dil, length, LANES)
    return pl.pallas_call(
        functools.partial(_attn_dil_kernel, dil=dil, length=length, half=half, kw=kw),
        grid=(b,),
        in_specs=[pl.BlockSpec((None, dil, length, 3 * W_B_OUT), lambda i: (i, 0, 0, 0)),
                  pl.BlockSpec(bias.shape, lambda i: (0, 0, 0, 0))],
        out_specs=pl.BlockSpec(blk, lambda i: (i, 0, 0, 0, 0, 0)),
        out_shape=jax.ShapeDtypeStruct((b, 2, npair, dil, length, LANES), F32),
        compiler_params=_params(("parallel",)),
        name=f"attn_dil{dil}",
    )(qkv, bias)


def _attn_mem_kernel(q_ref, mkv_ref, o_ref, *, seq):
    units = [(r, h * HEAD_DIM_M) for r in range(0, seq, QBLK) for h in range(N_HEADS_M)]

    def scores(u):
        r, c = units[u]
        return lax.dot_general(q_ref[r:r + QBLK, c:c + HEAD_DIM_M], mkv_ref[:, c:c + HEAD_DIM_M], _NT,
                               preferred_element_type=F32)

    pending = [scores(u) for u in range(LOOKAHEAD)]
    for u, (r, c) in enumerate(units):
        s = pending.pop(0)
        if u + LOOKAHEAD < len(units):
            pending.append(scores(u + LOOKAHEAD))
        p = jnp.exp2(s - jnp.max(s, axis=-1, keepdims=True)).astype(BF16)
        mv = mkv_ref[:, W_M + c:W_M + c + HEAD_DIM_M]
        o = jnp.dot(p, jnp.concatenate([mv, jnp.ones_like(mv)], axis=1), preferred_element_type=F32)
        o_ref[r:r + QBLK, c:c + HEAD_DIM_M] = (o[:, :HEAD_DIM_M] / o[:, HEAD_DIM_M:]).astype(o_ref.dtype)


def _attn_mem(qm, mkv, b, seq, n_mem):
    return pl.pallas_call(
        functools.partial(_attn_mem_kernel, seq=seq),
        grid=(b,),
        in_specs=[pl.BlockSpec((seq, W_M), lambda i: (i, 0)),
                  pl.BlockSpec((None, n_mem, 2 * W_M), lambda i: (i, 0, 0))],
        out_specs=pl.BlockSpec((seq, W_M), lambda i: (i, 0)),
        out_shape=jax.ShapeDtypeStruct((b * seq, W_M), BF16),
        compiler_params=_params(("parallel",)),
        name="attn_mem",
    )(qm, mkv.reshape(b, n_mem, 2 * W_M))


def _tail_kernel(x_ref, oa_ref, om_ref, ol0_ref, ol1_ref, ol2_ref,
                 g1_ref, wg_ref, wa_ref, wb_ref, wm_ref, wo_ref, g2_ref, wu_ref, wd_ref, gf_ref,
                 y_ref, mix_ref, *, final_norm):
    tm = x_ref.shape[0]
    half = tm // TAIL_CHAINS

    def dot(a, b):
        return jnp.dot(a, b, preferred_element_type=F32)

    def chain(j):
        rows = slice(j * half, (j + 1) * half)

        def slabs(ref, a, k):
            _, npair, dil, per, _ = ref.shape
            per //= TAIL_CHAINS
            if dil == 1:
                return jnp.concatenate([ref[a, p, 0, rows] for p in range(npair)], axis=1)
            for p in range(npair):
                for r in range(dil):
                    mix_ref[k, p, pl.ds(j * half + r, per, stride=dil), :] = ref[a, p, r, j * per:(j + 1) * per]
            return jnp.concatenate([mix_ref[k, p, rows] for p in range(npair)], axis=1)

        br_a = dot(oa_ref[rows], wa_ref[...])
        br_m = dot(om_ref[rows], wm_ref[...])
        yield
        x = x_ref[rows]
        h = _rms(x, g1_ref[...]).astype(BF16)
        g_a, g_b, g_m = (dot(h, wg_ref[:, k * D_MODEL:(k + 1) * D_MODEL]) for k in range(3))
        yield
        l0, l1, l2 = slabs(ol0_ref, 1, 0), slabs(ol1_ref, 1, 0), slabs(ol2_ref, 1, 1)
        mx = jnp.maximum(jnp.maximum(l0, l1), l2)
        e0, e1, e2 = jnp.exp2(l0 - mx), jnp.exp2(l1 - mx), jnp.exp2(l2 - mx)
        ob = (e0 * slabs(ol0_ref, 0, 0) + e1 * slabs(ol1_ref, 0, 2) + e2 * slabs(ol2_ref, 0, 3)) / (e0 + e1 + e2)
        br_b = dot(ob.astype(BF16), wb_ref[...])
        yield
        merged = jax.nn.sigmoid(g_a) * br_a + jax.nn.sigmoid(g_b) * br_b + jax.nn.sigmoid(g_m) * br_m
        x = x + dot(merged.astype(BF16), wo_ref[...])
        yield
        h2 = _rms(x, g2_ref[...]).astype(BF16)
        for c in range(D_FF // FF_CHUNK):
            u = dot(h2, wu_ref[:, c * FF_CHUNK:(c + 1) * FF_CHUNK])
            yield
            a = jnp.square(jnp.maximum(u, 0.0)).astype(BF16)
            x = x + dot(a, wd_ref[c * FF_CHUNK:(c + 1) * FF_CHUNK, :])
            yield
        if final_norm:
            x = _rms(x, gf_ref[...])
        y_ref[rows] = x

    done = object()
    waiting = [chain(j) for j in range(TAIL_CHAINS)]
    live, step = [], 0
    while live or waiting:
        if waiting and step % TAIL_SKEW == 0:
            live.append(waiting.pop(0))
        live = [g for g in live if next(g, done) is not done]
        step += 1


def _tail(x2d, oa, om, ols, g1, wg, wa, wb, wm, wo, g2, wu, wd, gf, tm, final_norm):
    t = x2d.shape[0]
    npair = HB_PER_GROUP // 2
    nt = ols[0].shape[3] * ols[0].shape[4] // tm

    def rows(width):
        return pl.BlockSpec((tm, width), lambda i: (i, 0))

    def whole(a):
        return pl.BlockSpec(a.shape, lambda i: (0, 0), pipeline_mode=pl.Buffered(1))

    def slab(a):
        dil = a.shape[3]
        return pl.BlockSpec((None, 2, npair, dil, tm // dil, LANES), lambda i: (i // nt, 0, 0, 0, i % nt, 0))

    consts = (g1, wg, wa, wb, wm, wo, g2, wu, wd, gf)
    return pl.pallas_call(
        functools.partial(_tail_kernel, final_norm=final_norm),
        grid=(t // tm,),
        in_specs=([rows(D_MODEL), rows(W_A), rows(W_M)] + [slab(a) for a in ols]
                  + [whole(a) for a in consts]),
        out_specs=rows(D_MODEL),
        out_shape=jax.ShapeDtypeStruct((t, D_MODEL), F32),
        scratch_shapes=[pltpu.VMEM((4, npair, tm, LANES), F32)],
        compiler_params=_params(("parallel",)),
        name="tail",
    )(x2d, oa, om, *ols, *consts)


def _layer(x, mem, rel_bias, norm1_g, w_in, mem_norm_g, w_mem_kv, sink_logit, w_branch_a,
           w_branch_b, w_branch_m, w_out, norm2_g, w_up, w_down, final_norm_g, final_norm):
    b, seq, _ = x.shape
    n_mem = mem.shape[1]
    t = b * seq
    x2d = x.reshape(t, D_MODEL)

    off = np.cumsum((0,) + IN_SPLITS)
    half_heads = N_HEADS_A // 2
    scale = HEAD_DIM ** -0.5 * LOG2E
    w_qa = (w_in[:, off[0]:off[1]].reshape(D_MODEL, 2, half_heads, HEAD_DIM)
            .transpose(0, 2, 1, 3).reshape(D_MODEL, W_A)) * scale
    parts = [w_qa, w_in[:, off[1]:off[3]]]
    for g in range(len(DILATED_GROUPS)):
        lo, hi = g * W_B_OUT, (g + 1) * W_B_OUT
        parts += [w_in[:, off[3] + lo:off[3] + hi] * scale, w_in[:, off[4] + lo:off[4] + hi],
                  w_in[:, off[5] + lo:off[5] + hi]]
    parts.append(w_in[:, off[6]:off[7]] * (HEAD_DIM_M ** -0.5 * LOG2E))
    w_qkv = jnp.concatenate(parts, axis=1).astype(BF16)
    plan = ((W_A, 1), (2 * W_KV_A, 1)) + tuple((3 * W_B_OUT, dil) for _, dil in DILATED_GROUPS) + ((W_M, 1),)
    qa, kva, qkv0, qkv1, qkv2, qm = _in_proj(x, norm1_g.reshape(1, D_MODEL), w_qkv, plan, tm=PROJ_ROWS)

    bias_a = _bias_tiles(rel_bias, np.arange(N_HEADS_A), WIN_A, 1, 3 * WIN_A, (0, -WIN_A, -2 * WIN_A))
    oa = _attn_win(sink_logit * LOG2E, qa, kva, bias_a, b, seq)

    ols = []
    for g, ((win, dil), qkv) in enumerate(zip(DILATED_GROUPS, (qkv0, qkv1, qkv2))):
        half = win // (2 * dil)
        length = seq // dil
        kw = min(4 * half, length)
        deltas = (0, -half, -2 * half) if length > kw else (0,)
        hcols = N_HEADS_A + g * HB_PER_GROUP + np.arange(HB_PER_GROUP)
        bias_g = _bias_tiles(rel_bias, hcols, half, dil, kw, deltas)
        ols.append(_attn_dil(qkv.reshape(b, dil, length, 3 * W_B_OUT), bias_g, dil, half))

    mkv, = _in_proj(mem.reshape(1, b * n_mem, D_MODEL), mem_norm_g.reshape(1, D_MODEL),
                    w_mem_kv.astype(BF16), ((2 * W_M, 1),), tm=CHAIN_ROWS)
    om = _attn_mem(qm, mkv, b, seq, n_mem)

    wa = (w_branch_a.reshape(2, half_heads, HEAD_DIM, D_MODEL).transpose(1, 0, 2, 3)
          .reshape(W_A, D_MODEL))
    y = _tail(x2d, oa, om, ols, norm1_g.reshape(1, D_MODEL), w_in[:, off[7]:].astype(BF16),
              wa.astype(BF16), w_branch_b.astype(BF16), w_branch_m.astype(BF16), w_out.astype(BF16),
              norm2_g.reshape(1, D_MODEL), w_up.astype(BF16), w_down.astype(BF16),
              final_norm_g.reshape(1, D_MODEL), tm=TAIL_ROWS, final_norm=final_norm)
    return y.reshape(b, seq, D_MODEL)


def _trunk(x, mem, rel_bias, norm1_g, w_in, mem_norm_g, w_mem_kv, sink_logit, w_branch_a,
           w_branch_b, w_branch_m, w_out, norm2_g, w_up, w_down, final_norm_g):
    depth = w_in.shape[0]
    for layer in range(depth):
        x = _layer(x, mem, rel_bias, norm1_g[layer], w_in[layer], mem_norm_g[layer], w_mem_kv[layer],
                   sink_logit[layer], w_branch_a[layer], w_branch_b[layer], w_branch_m[layer],
                   w_out[layer], norm2_g[layer], w_up[layer], w_down[layer], final_norm_g,
                   final_norm=layer == depth - 1)
    return x


def kernel(x_prompt, x_sample, mem_prompt, mem_sample, rel_bias, norm1_g, w_in, mem_norm_g, w_mem_kv,
           sink_logit, w_branch_a, w_branch_b, w_branch_m, w_out, norm2_g, w_up, w_down, final_norm_g):
    weights = (rel_bias, norm1_g, w_in, mem_norm_g, w_mem_kv, sink_logit, w_branch_a, w_branch_b,
               w_branch_m, w_out, norm2_g, w_up, w_down, final_norm_g)
    return (_trunk(x_prompt, mem_prompt, *weights), _trunk(x_sample, mem_sample, *weights))
```

```python
import functools

import numpy as np
import jax
import jax.numpy as jnp
from jax import lax
from jax.experimental import pallas as pl
from jax.experimental.pallas import tpu as pltpu

D_MODEL = 1024
HEAD_DIM = 64
N_HEADS_A = 8
N_KV_A = 2
WIN_A = 128
DILATED_GROUPS = ((128, 1), (512, 4), (2048, 16))
HB_PER_GROUP = 4
N_HEADS_B = HB_PER_GROUP * len(DILATED_GROUPS)
N_HEADS_M = 4
HEAD_DIM_M = 128
D_FF = 4 * D_MODEL
NUM_BUCKETS = 32
MAX_DIST = 1024
EPS = 1e-6
NEG = -1e30

W_A = N_HEADS_A * HEAD_DIM
W_KV_A = N_KV_A * HEAD_DIM
W_B = N_HEADS_B * HEAD_DIM
W_B_OUT = HB_PER_GROUP * HEAD_DIM
W_M = N_HEADS_M * HEAD_DIM_M
IN_SPLITS = (W_A, W_KV_A, W_KV_A, W_B, W_B, W_B, W_M, D_MODEL, D_MODEL, D_MODEL)

LANES = 128
QBLK = 128
LOOKAHEAD = 2
LOOKAHEAD_DIL = 5
LOOKAHEAD_DIL_NARROW = 6
LOG2E = 1.4426950408889634
FF_CHUNK = 2048
PROJ_CHUNK = 512
CHAIN_ROWS = 512
PROJ_ROWS = 2 * CHAIN_ROWS
TAIL_ROWS = 512
TAIL_CHAINS = 2
TAIL_SKEW = 3
VMEM_LIMIT = 60 * 1024 * 1024
BF16 = jnp.bfloat16
F32 = jnp.float32


def _t5_bucket(rel):
    half = NUM_BUCKETS // 2
    ret = (rel > 0).astype(np.int32) * half
    n = np.abs(rel)
    max_exact = half // 2
    large = max_exact + (np.log(np.maximum(n, 1) / max_exact) / np.log(MAX_DIST / max_exact)
                         * (half - max_exact)).astype(np.int32)
    large = np.minimum(large, half - 1)
    return (ret + np.where(n < max_exact, n, large)).astype(np.int32)


def _bias_tiles(rel_bias, cols, half, dil, kw, deltas):
    period = QBLK + kw
    k = np.arange(period)
    off = np.where(k < kw, k, k - period)[None, :] + np.asarray(deltas)[:, None]
    valid = np.abs(off) <= half
    bucket = _t5_bucket(dil * np.clip(off, -half, half))
    v = jnp.transpose(rel_bias[bucket][:, :, cols], (0, 2, 1))
    v = jnp.where(valid[:, None, :], v, NEG) * LOG2E
    t = jnp.tile(v, (1, 1, QBLK))[:, :, :QBLK * (period - 1)]
    return t.reshape(len(deltas), len(cols), QBLK, period - 1)[:, :, :, :kw]


def _params(sem):
    return pltpu.CompilerParams(dimension_semantics=sem, vmem_limit_bytes=VMEM_LIMIT)


def _rms(x, g):
    return x * lax.rsqrt(jnp.mean(x * x, axis=-1, keepdims=True) + EPS) * g


def _in_proj_kernel(x_ref, g_ref, w_ref, *refs, plan, tm, chains):
    o_refs = refs[:len(plan)]
    hs_ref = refs[len(plan)] if len(refs) > len(plan) else None
    half = tm // chains

    def chain(j):
        base = j * half
        h = _rms(x_ref[base:base + half], g_ref[...])
        hb = h.astype(BF16)
        if hs_ref is not None:
            for k in range(D_MODEL // LANES):
                hs_ref[k, base:base + half] = h[:, k * LANES:(k + 1) * LANES]
        c = 0
        for o_ref, (wd, dil) in zip(o_refs, plan):
            rows = half // dil
            if dil == 1:
                lhs = hb
            else:
                lhs = jnp.concatenate(
                    [jnp.concatenate([hs_ref[k, pl.ds(base + r, rows, stride=dil), :]
                                      for k in range(D_MODEL // LANES)], axis=1)
                     for r in range(dil)], axis=0).astype(BF16)
            for s in range(0, wd, PROJ_CHUNK):
                e = min(wd, s + PROJ_CHUNK)
                res = jnp.dot(lhs, w_ref[:, c + s:c + e], preferred_element_type=F32).astype(o_ref.dtype)
                yield
                if dil == 1:
                    o_ref[base:base + half, s:e] = res
                else:
                    for r in range(dil):
                        o_ref[r, j * rows:(j + 1) * rows, s:e] = res[r * rows:(r + 1) * rows]
            c += wd

    done = object()
    live = [chain(j) for j in range(chains)]
    while live:
        live = [g for g in live if next(g, done) is not done]


def _in_proj(x, g, w, plan, tm):
    b, seq, _ = x.shape
    nt = seq // tm
    out_specs, out_shape = [], []
    for wd, dil in plan:
        if dil == 1:
            out_specs.append(pl.BlockSpec((tm, wd), lambda i, j: (i * nt + j, 0)))
            out_shape.append(jax.ShapeDtypeStruct((b * seq, wd), BF16))
        else:
            out_specs.append(pl.BlockSpec((None, dil, tm // dil, wd), lambda i, j: (i, 0, j, 0)))
            out_shape.append(jax.ShapeDtypeStruct((b, dil, seq // dil, wd), BF16))
    dilated = any(dil > 1 for _, dil in plan)
    return pl.pallas_call(
        functools.partial(_in_proj_kernel, plan=plan, tm=tm, chains=tm // CHAIN_ROWS),
        grid=(b, nt),
        in_specs=[pl.BlockSpec((None, tm, D_MODEL), lambda i, j: (i, j, 0)),
                  pl.BlockSpec((1, D_MODEL), lambda i, j: (0, 0)),
                  pl.BlockSpec(w.shape, lambda i, j: (0, 0), pipeline_mode=pl.Buffered(1))],
        out_specs=out_specs,
        out_shape=out_shape,
        scratch_shapes=[pltpu.VMEM((D_MODEL // LANES, tm, LANES), F32)] if dilated else [],
        compiler_params=_params(("parallel", "arbitrary")),
        name="in_proj",
    )(x, g, w)


def _pair_stream(units, emit, lookahead):
    cache = {}

    def scores(u):
        cache[u] = units[u]()
        return cache[u][0]()

    def parts(s, sink):
        m = jnp.max(s, axis=-1, keepdims=True)
        if sink is not None:
            m = jnp.maximum(m, sink)
        return m, jnp.exp2(s - m).astype(BF16)

    pending = [scores(u) for u in range(min(lookahead, len(units)))]
    for u in range(len(units)):
        s_lo, s_hi = pending.pop(0)
        if u + lookahead < len(units):
            pending.append(scores(u + lookahead))
        _, value_fn, sinks = cache.pop(u)
        m_lo, p_lo = parts(s_lo, sinks and sinks[0])
        m_hi, p_hi = parts(s_hi, sinks and sinks[1])
        o, den = value_fn(p_lo, p_hi)
        low = lax.broadcasted_iota(jnp.int32, o.shape, 1) < HEAD_DIM
        m = jnp.where(low, m_lo, m_hi)
        if sinks:
            den = den + jnp.exp2(jnp.where(low, sinks[0], sinks[1]) - m)
        emit(u, o / den, m + jnp.log2(den))


def _half(x2, hi):
    lane = lax.broadcasted_iota(jnp.int32, x2.shape, 1)
    keep = lane >= HEAD_DIM if hi else lane < HEAD_DIM
    return jnp.where(keep, x2, jnp.zeros_like(x2))


def _window(n, nblk, half, length, kw):
    if nblk == 1:
        return 0, 0
    return min(max(n * QBLK - half, 0), length - kw), (0 if n == 0 else 2 if n == nblk - 1 else 1)


_NT = (((1,), (1,)), ((), ()))


def _attn_win_kernel(sink_ref, q_ref, kv_ref, bias_ref, ones_ref, o_ref, *, seq):
    nblk = seq // QBLK
    kw = 3 * WIN_A
    half_heads = N_HEADS_A // 2
    stacks = {}

    def stacked(n, ws):
        if n not in stacks:
            k2 = kv_ref[ws:ws + kw, 0:LANES]
            v2 = kv_ref[ws:ws + kw, LANES:2 * LANES]
            stacks.clear()
            stacks[n] = (jnp.concatenate([_half(k2, False), _half(k2, True)], axis=0),
                         jnp.concatenate([jnp.concatenate([_half(v2, False), _half(v2, True)], axis=0),
                                          ones_ref[...]], axis=1))
        return stacks[n]

    units, where = [], []
    for n in range(nblk):
        ws, var = _window(n, nblk, WIN_A, seq, kw)
        for t in range(half_heads):

            def unit(n=n, t=t, ws=ws, var=var):
                km, vm = stacked(n, ws)

                def score_fn():
                    s = lax.dot_general(q_ref[n * QBLK:(n + 1) * QBLK, t * LANES:(t + 1) * LANES], km,
                                        _NT, preferred_element_type=F32)
                    return s[:, :kw] + bias_ref[var, t], s[:, kw:] + bias_ref[var, half_heads + t]

                def value_fn(p_lo, p_hi):
                    o = jnp.dot(jnp.concatenate([p_lo, p_hi], axis=1), vm, preferred_element_type=F32)
                    return o[:, :LANES], o[:, LANES:]

                return score_fn, value_fn, (sink_ref[t], sink_ref[half_heads + t])

            units.append(unit)
            where.append((n, t))

    def emit(u, o, _):
        n, t = where[u]
        o_ref[n * QBLK:(n + 1) * QBLK, t * LANES:(t + 1) * LANES] = o.astype(o_ref.dtype)

    _pair_stream(units, emit, LOOKAHEAD)


def _attn_win(sink, qa, kva, bias, b, seq):
    kw = bias.shape[-1]
    ones = np.zeros((2 * kw, LANES), np.float32)
    ones[:kw, :HEAD_DIM] = 1.0
    ones[kw:, HEAD_DIM:] = 1.0
    return pl.pallas_call(
        functools.partial(_attn_win_kernel, seq=seq),
        grid=(b,),
        in_specs=[pl.BlockSpec(memory_space=pltpu.SMEM),
                  pl.BlockSpec((seq, W_A), lambda i: (i, 0)),
                  pl.BlockSpec((seq, 2 * LANES), lambda i: (i, 0)),
                  pl.BlockSpec(bias.shape, lambda i: (0, 0, 0, 0)),
                  pl.BlockSpec(ones.shape, lambda i: (0, 0))],
        out_specs=pl.BlockSpec((seq, W_A), lambda i: (i, 0)),
        out_shape=jax.ShapeDtypeStruct((b * seq, W_A), BF16),
        compiler_params=_params(("parallel",)),
        name="attn_win",
    )(sink, qa, kva, bias, jnp.asarray(ones, BF16))


def _attn_dil_kernel(qkv_ref, bias_ref, o_ref, l_ref, *, dil, length, half, kw):
    nblk = length // QBLK
    units, where = [], []
    for r in range(dil):
        for n in range(nblk):
            ws, var = _window(n, nblk, half, length, kw)
            for t in range(HB_PER_GROUP // 2):
                c = t * LANES

                def unit(r=r, n=n, c=c, t=t, ws=ws, var=var):
                    def score_fn():
                        q2 = qkv_ref[r, n * QBLK:(n + 1) * QBLK, c:c + LANES]
                        k2 = qkv_ref[r, ws:ws + kw, W_B_OUT + c:W_B_OUT + c + LANES]
                        return tuple(lax.dot_general(_half(q2, hi), k2, _NT, preferred_element_type=F32)
                                     + bias_ref[var, 2 * t + hi] for hi in (0, 1))

                    def value_fn(p_lo, p_hi):
                        v2 = qkv_ref[r, ws:ws + kw, 2 * W_B_OUT + c:2 * W_B_OUT + c + LANES]
                        v2 = jnp.concatenate([v2, jnp.ones_like(v2)], axis=1)
                        o_lo = jnp.dot(p_lo, v2, preferred_element_type=F32)
                        o_hi = jnp.dot(p_hi, v2, preferred_element_type=F32)
                        low = lax.broadcasted_iota(jnp.int32, o_lo.shape, 1) % LANES < HEAD_DIM
                        o = jnp.where(low, o_lo, o_hi)
                        return o[:, :LANES], o[:, LANES:]

                    return score_fn, value_fn, None

                units.append(unit)
                where.append((r, n, t))

    def emit(u, o, lse2):
        r, n, t = where[u]
        o_ref[t, r, n * QBLK:(n + 1) * QBLK, :] = o.astype(o_ref.dtype)
        l_ref[t, r, n * QBLK:(n + 1) * QBLK, :] = lse2

    _pair_stream(units, emit, LOOKAHEAD_DIL if kw > QBLK else LOOKAHEAD_DIL_NARROW)


def _attn_dil(qkv, bias, dil, half):
    b, _, length, _ = qkv.shape
    kw = bias.shape[-1]
    npair = HB_PER_GROUP // 2
    blk = (None, npair, dil, length, LANES)
    return pl.pallas_call(
        functools.partial(_attn_dil_kernel, dil=dil, length=length, half=half, kw=kw),
        grid=(b,),
        in_specs=[pl.BlockSpec((None, dil, length, 3 * W_B_OUT), lambda i: (i, 0, 0, 0)),
                  pl.BlockSpec(bias.shape, lambda i: (0, 0, 0, 0))],
        out_specs=[pl.BlockSpec(blk, lambda i: (i, 0, 0, 0, 0))] * 2,
        out_shape=[jax.ShapeDtypeStruct((b, npair, dil, length, LANES), dt) for dt in (BF16, F32)],
        compiler_params=_params(("parallel",)),
        name=f"attn_dil{dil}",
    )(qkv, bias)


def _attn_mem_kernel(q_ref, mkv_ref, o_ref, *, seq):
    units = [(r, h * HEAD_DIM_M) for r in range(0, seq, QBLK) for h in range(N_HEADS_M)]

    def scores(u):
        r, c = units[u]
        return lax.dot_general(q_ref[r:r + QBLK, c:c + HEAD_DIM_M], mkv_ref[:, c:c + HEAD_DIM_M], _NT,
                               preferred_element_type=F32)

    pending = [scores(u) for u in range(LOOKAHEAD)]
    for u, (r, c) in enumerate(units):
        s = pending.pop(0)
        if u + LOOKAHEAD < len(units):
            pending.append(scores(u + LOOKAHEAD))
        p = jnp.exp2(s - jnp.max(s, axis=-1, keepdims=True)).astype(BF16)
        mv = mkv_ref[:, W_M + c:W_M + c + HEAD_DIM_M]
        o = jnp.dot(p, jnp.concatenate([mv, jnp.ones_like(mv)], axis=1), preferred_element_type=F32)
        o_ref[r:r + QBLK, c:c + HEAD_DIM_M] = (o[:, :HEAD_DIM_M] / o[:, HEAD_DIM_M:]).astype(o_ref.dtype)


def _attn_mem(qm, mkv, b, seq, n_mem):
    return pl.pallas_call(
        functools.partial(_attn_mem_kernel, seq=seq),
        grid=(b,),
        in_specs=[pl.BlockSpec((seq, W_M), lambda i: (i, 0)),
                  pl.BlockSpec((None, n_mem, 2 * W_M), lambda i: (i, 0, 0))],
        out_specs=pl.BlockSpec((seq, W_M), lambda i: (i, 0)),
        out_shape=jax.ShapeDtypeStruct((b * seq, W_M), BF16),
        compiler_params=_params(("parallel",)),
        name="attn_mem",
    )(qm, mkv.reshape(b, n_mem, 2 * W_M))


def _tail_kernel(x_ref, oa_ref, om_ref, o0_ref, o1_ref, o2_ref, l0_ref, l1_ref, l2_ref,
                 g1_ref, wg_ref, wa_ref, wb_ref, wm_ref, wo_ref, g2_ref, wu_ref, wd_ref, gf_ref,
                 y_ref, mix_ref, *, final_norm):
    tm = x_ref.shape[0]
    half = tm // TAIL_CHAINS

    def dot(a, b):
        return jnp.dot(a, b, preferred_element_type=F32)

    def chain(j):
        rows = slice(j * half, (j + 1) * half)

        def slabs(ref, k):
            npair, dil, per, _ = ref.shape
            per //= TAIL_CHAINS
            if dil == 1:
                return jnp.concatenate([ref[p, 0, rows].astype(F32) for p in range(npair)], axis=1)
            for p in range(npair):
                for r in range(dil):
                    mix_ref[k, p, pl.ds(j * half + r, per, stride=dil), :] = (
                        ref[p, r, j * per:(j + 1) * per].astype(F32))
            return jnp.concatenate([mix_ref[k, p, rows] for p in range(npair)], axis=1)

        br_a = dot(oa_ref[rows], wa_ref[...])
        br_m = dot(om_ref[rows], wm_ref[...])
        yield
        x = x_ref[rows]
        h = _rms(x, g1_ref[...]).astype(BF16)
        g_a, g_b, g_m = (dot(h, wg_ref[:, k * D_MODEL:(k + 1) * D_MODEL]) for k in range(3))
        yield
        l0, l1, l2 = slabs(l0_ref, 0), slabs(l1_ref, 0), slabs(l2_ref, 1)
        mx = jnp.maximum(jnp.maximum(l0, l1), l2)
        e0, e1, e2 = jnp.exp2(l0 - mx), jnp.exp2(l1 - mx), jnp.exp2(l2 - mx)
        ob = (e0 * slabs(o0_ref, 0) + e1 * slabs(o1_ref, 2) + e2 * slabs(o2_ref, 3)) / (e0 + e1 + e2)
        br_b = dot(ob.astype(BF16), wb_ref[...])
        yield
        merged = jax.nn.sigmoid(g_a) * br_a + jax.nn.sigmoid(g_b) * br_b + jax.nn.sigmoid(g_m) * br_m
        x = x + dot(merged.astype(BF16), wo_ref[...])
        yield
        h2 = _rms(x, g2_ref[...]).astype(BF16)
        for c in range(D_FF // FF_CHUNK):
            u = dot(h2, wu_ref[:, c * FF_CHUNK:(c + 1) * FF_CHUNK])
            yield
            a = jnp.square(jnp.maximum(u, 0.0)).astype(BF16)
            x = x + dot(a, wd_ref[c * FF_CHUNK:(c + 1) * FF_CHUNK, :])
            yield
        if final_norm:
            x = _rms(x, gf_ref[...])
        y_ref[rows] = x

    done = object()
    waiting = [chain(j) for j in range(TAIL_CHAINS)]
    live, step = [], 0
    while live or waiting:
        if waiting and step % TAIL_SKEW == 0:
            live.append(waiting.pop(0))
        live = [g for g in live if next(g, done) is not done]
        step += 1


def _tail(x2d, oa, om, obs, lbs, g1, wg, wa, wb, wm, wo, g2, wu, wd, gf, tm, final_norm):
    t = x2d.shape[0]
    npair = HB_PER_GROUP // 2
    nt = obs[0].shape[2] * obs[0].shape[3] // tm

    def rows(width):
        return pl.BlockSpec((tm, width), lambda i: (i, 0))

    def whole(a):
        return pl.BlockSpec(a.shape, lambda i: (0, 0), pipeline_mode=pl.Buffered(1))

    def slab(a):
        dil = a.shape[2]
        return pl.BlockSpec((None, npair, dil, tm // dil, LANES), lambda i: (i // nt, 0, 0, i % nt, 0))

    consts = (g1, wg, wa, wb, wm, wo, g2, wu, wd, gf)
    return pl.pallas_call(
        functools.partial(_tail_kernel, final_norm=final_norm),
        grid=(t // tm,),
        in_specs=([rows(D_MODEL), rows(W_A), rows(W_M)] + [slab(a) for a in (*obs, *lbs)]
                  + [whole(a) for a in consts]),
        out_specs=rows(D_MODEL),
        out_shape=jax.ShapeDtypeStruct((t, D_MODEL), F32),
        scratch_shapes=[pltpu.VMEM((4, npair, tm, LANES), F32)],
        compiler_params=_params(("parallel",)),
        name="tail",
    )(x2d, oa, om, *obs, *lbs, *consts)


def _layer(x, mem, rel_bias, norm1_g, w_in, mem_norm_g, w_mem_kv, sink_logit, w_branch_a,
           w_branch_b, w_branch_m, w_out, norm2_g, w_up, w_down, final_norm_g, final_norm):
    b, seq, _ = x.shape
    n_mem = mem.shape[1]
    t = b * seq
    x2d = x.reshape(t, D_MODEL)

    off = np.cumsum((0,) + IN_SPLITS)
    half_heads = N_HEADS_A // 2
    scale = HEAD_DIM ** -0.5 * LOG2E
    w_qa = (w_in[:, off[0]:off[1]].reshape(D_MODEL, 2, half_heads, HEAD_DIM)
            .transpose(0, 2, 1, 3).reshape(D_MODEL, W_A)) * scale
    parts = [w_qa, w_in[:, off[1]:off[3]]]
    for g in range(len(DILATED_GROUPS)):
        lo, hi = g * W_B_OUT, (g + 1) * W_B_OUT
        parts += [w_in[:, off[3] + lo:off[3] + hi] * scale, w_in[:, off[4] + lo:off[4] + hi],
                  w_in[:, off[5] + lo:off[5] + hi]]
    parts.append(w_in[:, off[6]:off[7]] * (HEAD_DIM_M ** -0.5 * LOG2E))
    w_qkv = jnp.concatenate(parts, axis=1).astype(BF16)
    plan = ((W_A, 1), (2 * W_KV_A, 1)) + tuple((3 * W_B_OUT, dil) for _, dil in DILATED_GROUPS) + ((W_M, 1),)
    qa, kva, qkv0, qkv1, qkv2, qm = _in_proj(x, norm1_g.reshape(1, D_MODEL), w_qkv, plan, tm=PROJ_ROWS)

    bias_a = _bias_tiles(rel_bias, np.arange(N_HEADS_A), WIN_A, 1, 3 * WIN_A, (0, -WIN_A, -2 * WIN_A))
    oa = _attn_win(sink_logit * LOG2E, qa, kva, bias_a, b, seq)

    obs, lbs = [], []
    for g, ((win, dil), qkv) in enumerate(zip(DILATED_GROUPS, (qkv0, qkv1, qkv2))):
        half = win // (2 * dil)
        length = seq // dil
        kw = min(4 * half, length)
        deltas = (0, -half, -2 * half) if length > kw else (0,)
        hcols = N_HEADS_A + g * HB_PER_GROUP + np.arange(HB_PER_GROUP)
        bias_g = _bias_tiles(rel_bias, hcols, half, dil, kw, deltas)
        o, l = _attn_dil(qkv.reshape(b, dil, length, 3 * W_B_OUT), bias_g, dil, half)
        obs.append(o)
        lbs.append(l)

    mkv, = _in_proj(mem.reshape(1, b * n_mem, D_MODEL), mem_norm_g.reshape(1, D_MODEL),
                    w_mem_kv.astype(BF16), ((2 * W_M, 1),), tm=CHAIN_ROWS)
    om = _attn_mem(qm, mkv, b, seq, n_mem)

    wa = (w_branch_a.reshape(2, half_heads, HEAD_DIM, D_MODEL).transpose(1, 0, 2, 3)
          .reshape(W_A, D_MODEL))
    y = _tail(x2d, oa, om, obs, lbs, norm1_g.reshape(1, D_MODEL), w_in[:, off[7]:].astype(BF16),
              wa.astype(BF16), w_branch_b.astype(BF16), w_branch_m.astype(BF16), w_out.astype(BF16),
              norm2_g.reshape(1, D_MODEL), w_up.astype(BF16), w_down.astype(BF16),
              final_norm_g.reshape(1, D_MODEL), tm=TAIL_ROWS, final_norm=final_norm)
    return y.reshape(b, seq, D_MODEL)


def _trunk(x, mem, rel_bias, norm1_g, w_in, mem_norm_g, w_mem_kv, sink_logit, w_branch_a,
           w_branch_b, w_branch_m, w_out, norm2_g, w_up, w_down, final_norm_g):
    depth = w_in.shape[0]
    for layer in range(depth):
        x = _layer(x, mem, rel_bias, norm1_g[layer], w_in[layer], mem_norm_g[layer], w_mem_kv[layer],
                   sink_logit[layer], w_branch_a[layer], w_branch_b[layer], w_branch_m[layer],
                   w_out[layer], norm2_g[layer], w_up[layer], w_down[layer], final_norm_g,
                   final_norm=layer == depth - 1)
    return x


def kernel(x_prompt, x_sample, mem_prompt, mem_sample, rel_bias, norm1_g, w_in, mem_norm_g, w_mem_kv,
           sink_logit, w_branch_a, w_branch_b, w_branch_m, w_out, norm2_g, w_up, w_down, final_norm_g):
    weights = (rel_bias, norm1_g, w_in, mem_norm_g, w_mem_kv, sink_logit, w_branch_a, w_branch_b,
               w_branch_m, w_out, norm2_g, w_up, w_down, final_norm_g)
    return (_trunk(x_prompt, mem_prompt, *weights), _trunk(x_sample, mem_sample, *weights))
```

```python
import functools

import numpy as np
import jax
import jax.numpy as jnp
from jax import lax
from jax.experimental import pallas as pl
from jax.experimental.pallas import tpu as pltpu

D_MODEL = 1024
HEAD_DIM = 64
N_HEADS_A = 8
N_KV_A = 2
WIN_A = 128
DILATED_GROUPS = ((128, 1), (512, 4), (2048, 16))
HB_PER_GROUP = 4
N_HEADS_B = HB_PER_GROUP * len(DILATED_GROUPS)
N_HEADS_M = 4
HEAD_DIM_M = 128
D_FF = 4 * D_MODEL
NUM_BUCKETS = 32
MAX_DIST = 1024
EPS = 1e-6
NEG = -1e30

W_A = N_HEADS_A * HEAD_DIM
W_KV_A = N_KV_A * HEAD_DIM
W_B = N_HEADS_B * HEAD_DIM
W_B_OUT = HB_PER_GROUP * HEAD_DIM
W_M = N_HEADS_M * HEAD_DIM_M
IN_SPLITS = (W_A, W_KV_A, W_KV_A, W_B, W_B, W_B, W_M, D_MODEL, D_MODEL, D_MODEL)

LANES = 128
QBLK = 128
LOOKAHEAD = 2
LOOKAHEAD_DIL = 5
LOOKAHEAD_DIL_NARROW = 6
LOG2E = 1.4426950408889634
FF_CHUNK = 2048
PROJ_CHUNK = 512
CHAIN_ROWS = 512
PROJ_ROWS = 2 * CHAIN_ROWS
TAIL_ROWS = 512
TAIL_CHAINS = 2
TAIL_SKEW = 3
VMEM_LIMIT = 60 * 1024 * 1024
BF16 = jnp.bfloat16
F32 = jnp.float32


def _t5_bucket(rel):
    half = NUM_BUCKETS // 2
    ret = (rel > 0).astype(np.int32) * half
    n = np.abs(rel)
    max_exact = half // 2
    large = max_exact + (np.log(np.maximum(n, 1) / max_exact) / np.log(MAX_DIST / max_exact)
                         * (half - max_exact)).astype(np.int32)
    large = np.minimum(large, half - 1)
    return (ret + np.where(n < max_exact, n, large)).astype(np.int32)


def _bias_tiles(rel_bias, cols, half, dil, kw, deltas):
    period = QBLK + kw
    k = np.arange(period)
    off = np.where(k < kw, k, k - period)[None, :] + np.asarray(deltas)[:, None]
    valid = np.abs(off) <= half
    bucket = _t5_bucket(dil * np.clip(off, -half, half))
    v = jnp.transpose(rel_bias[bucket][:, :, cols], (0, 2, 1))
    v = jnp.where(valid[:, None, :], v, NEG) * LOG2E
    t = jnp.tile(v, (1, 1, QBLK))[:, :, :QBLK * (period - 1)]
    return t.reshape(len(deltas), len(cols), QBLK, period - 1)[:, :, :, :kw]


def _params(sem):
    return pltpu.CompilerParams(dimension_semantics=sem, vmem_limit_bytes=VMEM_LIMIT)


def _rms(x, g):
    return x * lax.rsqrt(jnp.mean(x * x, axis=-1, keepdims=True) + EPS) * g


def _in_proj_kernel(x_ref, g_ref, w_ref, *refs, plan, tm, chains):
    o_refs = refs[:len(plan)]
    hs_ref = refs[len(plan)] if len(refs) > len(plan) else None
    half = tm // chains

    def chain(j):
        base = j * half
        h = _rms(x_ref[base:base + half], g_ref[...])
        hb = h.astype(BF16)
        if hs_ref is not None:
            for k in range(D_MODEL // LANES):
                hs_ref[k, base:base + half] = h[:, k * LANES:(k + 1) * LANES]
        c = 0
        for o_ref, (wd, dil) in zip(o_refs, plan):
            rows = half // dil
            if dil == 1:
                lhs = hb
            else:
                lhs = jnp.concatenate(
                    [jnp.concatenate([hs_ref[k, pl.ds(base + r, rows, stride=dil), :]
                                      for k in range(D_MODEL // LANES)], axis=1)
                     for r in range(dil)], axis=0).astype(BF16)
            for s in range(0, wd, PROJ_CHUNK):
                e = min(wd, s + PROJ_CHUNK)
                res = jnp.dot(lhs, w_ref[:, c + s:c + e], preferred_element_type=F32).astype(o_ref.dtype)
                yield
                if dil == 1:
                    o_ref[base:base + half, s:e] = res
                else:
                    for r in range(dil):
                        o_ref[r, j * rows:(j + 1) * rows, s:e] = res[r * rows:(r + 1) * rows]
            c += wd

    done = object()
    live = [chain(j) for j in range(chains)]
    while live:
        live = [g for g in live if next(g, done) is not done]


def _in_proj(x, g, w, plan, tm):
    b, seq, _ = x.shape
    nt = seq // tm
    out_specs, out_shape = [], []
    for wd, dil in plan:
        if dil == 1:
            out_specs.append(pl.BlockSpec((tm, wd), lambda i, j: (i * nt + j, 0)))
            out_shape.append(jax.ShapeDtypeStruct((b * seq, wd), BF16))
        else:
            out_specs.append(pl.BlockSpec((None, dil, tm // dil, wd), lambda i, j: (i, 0, j, 0)))
            out_shape.append(jax.ShapeDtypeStruct((b, dil, seq // dil, wd), BF16))
    dilated = any(dil > 1 for _, dil in plan)
    return pl.pallas_call(
        functools.partial(_in_proj_kernel, plan=plan, tm=tm, chains=tm // CHAIN_ROWS),
        grid=(b, nt),
        in_specs=[pl.BlockSpec((None, tm, D_MODEL), lambda i, j: (i, j, 0)),
                  pl.BlockSpec((1, D_MODEL), lambda i, j: (0, 0)),
                  pl.BlockSpec(w.shape, lambda i, j: (0, 0), pipeline_mode=pl.Buffered(1))],
        out_specs=out_specs,
        out_shape=out_shape,
        scratch_shapes=[pltpu.VMEM((D_MODEL // LANES, tm, LANES), F32)] if dilated else [],
        compiler_params=_params(("parallel", "arbitrary")),
        name="in_proj",
    )(x, g, w)


def _pair_stream_steps(units, emit, lookahead):
    cache = {}

    def scores(u):
        cache[u] = units[u]()
        return cache[u][0]()

    def parts(s, sink):
        m = jnp.max(s, axis=-1, keepdims=True)
        if sink is not None:
            m = jnp.maximum(m, sink)
        return m, jnp.exp2(s - m).astype(BF16)

    pending = [scores(u) for u in range(min(lookahead, len(units)))]
    for u in range(len(units)):
        s_lo, s_hi = pending.pop(0)
        if u + lookahead < len(units):
            pending.append(scores(u + lookahead))
        _, value_fn, sinks = cache.pop(u)
        m_lo, p_lo = parts(s_lo, sinks and sinks[0])
        m_hi, p_hi = parts(s_hi, sinks and sinks[1])
        o, den = value_fn(p_lo, p_hi)
        low = lax.broadcasted_iota(jnp.int32, o.shape, 1) < HEAD_DIM
        m = jnp.where(low, m_lo, m_hi)
        if sinks:
            den = den + jnp.exp2(jnp.where(low, sinks[0], sinks[1]) - m)
        emit(u, o / den, m + jnp.log2(den))
        yield


def _pair_stream(units, emit, lookahead):
    for _ in _pair_stream_steps(units, emit, lookahead):
        pass


def _half(x2, hi):
    lane = lax.broadcasted_iota(jnp.int32, x2.shape, 1)
    keep = lane >= HEAD_DIM if hi else lane < HEAD_DIM
    return jnp.where(keep, x2, jnp.zeros_like(x2))


def _window(n, nblk, half, length, kw):
    if nblk == 1:
        return 0, 0
    return min(max(n * QBLK - half, 0), length - kw), (0 if n == 0 else 2 if n == nblk - 1 else 1)


_NT = (((1,), (1,)), ((), ()))


def _attn_win_steps(sink_ref, q_ref, kv_ref, bias_ref, ones_ref, o_ref, seq):
    nblk = seq // QBLK
    kw = 3 * WIN_A
    half_heads = N_HEADS_A // 2
    stacks = {}

    def stacked(n, ws):
        if n not in stacks:
            k2 = kv_ref[ws:ws + kw, 0:LANES]
            v2 = kv_ref[ws:ws + kw, LANES:2 * LANES]
            stacks.clear()
            stacks[n] = (jnp.concatenate([_half(k2, False), _half(k2, True)], axis=0),
                         jnp.concatenate([jnp.concatenate([_half(v2, False), _half(v2, True)], axis=0),
                                          ones_ref[...]], axis=1))
        return stacks[n]

    units, where = [], []
    for n in range(nblk):
        ws, var = _window(n, nblk, WIN_A, seq, kw)
        for t in range(half_heads):

            def unit(n=n, t=t, ws=ws, var=var):
                km, vm = stacked(n, ws)

                def score_fn():
                    s = lax.dot_general(q_ref[n * QBLK:(n + 1) * QBLK, t * LANES:(t + 1) * LANES], km,
                                        _NT, preferred_element_type=F32)
                    return s[:, :kw] + bias_ref[var, t], s[:, kw:] + bias_ref[var, half_heads + t]

                def value_fn(p_lo, p_hi):
                    o = jnp.dot(jnp.concatenate([p_lo, p_hi], axis=1), vm, preferred_element_type=F32)
                    return o[:, :LANES], o[:, LANES:]

                return score_fn, value_fn, (sink_ref[t], sink_ref[half_heads + t])

            units.append(unit)
            where.append((n, t))

    def emit(u, o, _):
        n, t = where[u]
        o_ref[n * QBLK:(n + 1) * QBLK, t * LANES:(t + 1) * LANES] = o.astype(o_ref.dtype)

    return _pair_stream_steps(units, emit, LOOKAHEAD)


def _attn_win_mem_kernel(sink_ref, q_ref, kv_ref, bias_ref, ones_ref, qm_ref, mkv_ref, oa_ref, om_ref, *, seq):
    for steps in (_attn_win_steps(sink_ref, q_ref, kv_ref, bias_ref, ones_ref, oa_ref, seq),
                  _attn_mem_steps(qm_ref, mkv_ref, om_ref, seq)):
        for _ in steps:
            pass


def _attn_win_mem(sink, qa, kva, bias, qm, mkv, b, seq, n_mem):
    kw = bias.shape[-1]
    ones = np.zeros((2 * kw, LANES), np.float32)
    ones[:kw, :HEAD_DIM] = 1.0
    ones[kw:, HEAD_DIM:] = 1.0
    return pl.pallas_call(
        functools.partial(_attn_win_mem_kernel, seq=seq),
        grid=(b,),
        in_specs=[pl.BlockSpec(memory_space=pltpu.SMEM),
                  pl.BlockSpec((seq, W_A), lambda i: (i, 0)),
                  pl.BlockSpec((seq, 2 * LANES), lambda i: (i, 0)),
                  pl.BlockSpec(bias.shape, lambda i: (0, 0, 0, 0)),
                  pl.BlockSpec(ones.shape, lambda i: (0, 0)),
                  pl.BlockSpec((seq, W_M), lambda i: (i, 0)),
                  pl.BlockSpec((None, n_mem, 2 * W_M), lambda i: (i, 0, 0))],
        out_specs=[pl.BlockSpec((seq, W_A), lambda i: (i, 0)), pl.BlockSpec((seq, W_M), lambda i: (i, 0))],
        out_shape=[jax.ShapeDtypeStruct((b * seq, W_A), BF16), jax.ShapeDtypeStruct((b * seq, W_M), BF16)],
        compiler_params=_params(("parallel",)),
        name="attn_win_mem",
    )(sink, qa, kva, bias, jnp.asarray(ones, BF16), qm, mkv.reshape(b, n_mem, 2 * W_M))


def _attn_dil_kernel(qkv_ref, bias_ref, o_ref, l_ref, *, dil, length, half, kw):
    nblk = length // QBLK
    units, where = [], []
    for r in range(dil):
        for n in range(nblk):
            ws, var = _window(n, nblk, half, length, kw)
            for t in range(HB_PER_GROUP // 2):
                c = t * LANES

                def unit(r=r, n=n, c=c, t=t, ws=ws, var=var):
                    def score_fn():
                        q2 = qkv_ref[r, n * QBLK:(n + 1) * QBLK, c:c + LANES]
                        k2 = qkv_ref[r, ws:ws + kw, W_B_OUT + c:W_B_OUT + c + LANES]
                        return tuple(lax.dot_general(_half(q2, hi), k2, _NT, preferred_element_type=F32)
                                     + bias_ref[var, 2 * t + hi] for hi in (0, 1))

                    def value_fn(p_lo, p_hi):
                        v2 = qkv_ref[r, ws:ws + kw, 2 * W_B_OUT + c:2 * W_B_OUT + c + LANES]
                        v2 = jnp.concatenate([v2, jnp.ones_like(v2)], axis=1)
                        o_lo = jnp.dot(p_lo, v2, preferred_element_type=F32)
                        o_hi = jnp.dot(p_hi, v2, preferred_element_type=F32)
                        low = lax.broadcasted_iota(jnp.int32, o_lo.shape, 1) % LANES < HEAD_DIM
                        o = jnp.where(low, o_lo, o_hi)
                        return o[:, :LANES], o[:, LANES:]

                    return score_fn, value_fn, None

                units.append(unit)
                where.append((r, n, t))

    def emit(u, o, lse2):
        r, n, t = where[u]
        o_ref[t, r, n * QBLK:(n + 1) * QBLK, :] = o.astype(o_ref.dtype)
        l_ref[t, r, n * QBLK:(n + 1) * QBLK, :] = lse2

    _pair_stream(units, emit, LOOKAHEAD_DIL if kw > QBLK else LOOKAHEAD_DIL_NARROW)


def _attn_dil(qkv, bias, dil, half):
    b, _, length, _ = qkv.shape
    kw = bias.shape[-1]
    npair = HB_PER_GROUP // 2
    blk = (None, npair, dil, length, LANES)
    return pl.pallas_call(
        functools.partial(_attn_dil_kernel, dil=dil, length=length, half=half, kw=kw),
        grid=(b,),
        in_specs=[pl.BlockSpec((None, dil, length, 3 * W_B_OUT), lambda i: (i, 0, 0, 0)),
                  pl.BlockSpec(bias.shape, lambda i: (0, 0, 0, 0))],
        out_specs=[pl.BlockSpec(blk, lambda i: (i, 0, 0, 0, 0))] * 2,
        out_shape=[jax.ShapeDtypeStruct((b, npair, dil, length, LANES), dt) for dt in (BF16, F32)],
        compiler_params=_params(("parallel",)),
        name=f"attn_dil{dil}",
    )(qkv, bias)


def _attn_mem_steps(q_ref, mkv_ref, o_ref, seq):
    units = [(r, h * HEAD_DIM_M) for r in range(0, seq, QBLK) for h in range(N_HEADS_M)]

    def scores(u):
        r, c = units[u]
        return lax.dot_general(q_ref[r:r + QBLK, c:c + HEAD_DIM_M], mkv_ref[:, c:c + HEAD_DIM_M], _NT,
                               preferred_element_type=F32)

    pending = [scores(u) for u in range(LOOKAHEAD)]
    for u, (r, c) in enumerate(units):
        s = pending.pop(0)
        if u + LOOKAHEAD < len(units):
            pending.append(scores(u + LOOKAHEAD))
        p = jnp.exp2(s - jnp.max(s, axis=-1, keepdims=True)).astype(BF16)
        mv = mkv_ref[:, W_M + c:W_M + c + HEAD_DIM_M]
        o = jnp.dot(p, jnp.concatenate([mv, jnp.ones_like(mv)], axis=1), preferred_element_type=F32)
        o_ref[r:r + QBLK, c:c + HEAD_DIM_M] = (o[:, :HEAD_DIM_M] / o[:, HEAD_DIM_M:]).astype(o_ref.dtype)
        yield


def _tail_kernel(x_ref, oa_ref, om_ref, o0_ref, o1_ref, o2_ref, l0_ref, l1_ref, l2_ref,
                 g1_ref, wg_ref, wa_ref, wb_ref, wm_ref, wo_ref, g2_ref, wu_ref, wd_ref, gf_ref,
                 y_ref, mix_ref, *, final_norm):
    tm = x_ref.shape[0]
    half = tm // TAIL_CHAINS

    def dot(a, b):
        return jnp.dot(a, b, preferred_element_type=F32)

    def chain(j):
        rows = slice(j * half, (j + 1) * half)

        def slabs(ref, k):
            npair, dil, per, _ = ref.shape
            per //= TAIL_CHAINS
            if dil == 1:
                return jnp.concatenate([ref[p, 0, rows].astype(F32) for p in range(npair)], axis=1)
            for p in range(npair):
                for r in range(dil):
                    mix_ref[k, p, pl.ds(j * half + r, per, stride=dil), :] = (
                        ref[p, r, j * per:(j + 1) * per].astype(F32))
            return jnp.concatenate([mix_ref[k, p, rows] for p in range(npair)], axis=1)

        br_a = dot(oa_ref[rows], wa_ref[...])
        br_m = dot(om_ref[rows], wm_ref[...])
        yield
        x = x_ref[rows]
        h = _rms(x, g1_ref[...]).astype(BF16)
        g_a, g_b, g_m = (dot(h, wg_ref[:, k * D_MODEL:(k + 1) * D_MODEL]) for k in range(3))
        yield
        l0, l1, l2 = slabs(l0_ref, 0), slabs(l1_ref, 0), slabs(l2_ref, 1)
        mx = jnp.maximum(jnp.maximum(l0, l1), l2)
        e0, e1, e2 = jnp.exp2(l0 - mx), jnp.exp2(l1 - mx), jnp.exp2(l2 - mx)
        ob = (e0 * slabs(o0_ref, 0) + e1 * slabs(o1_ref, 2) + e2 * slabs(o2_ref, 3)) / (e0 + e1 + e2)
        br_b = dot(ob.astype(BF16), wb_ref[...])
        yield
        merged = jax.nn.sigmoid(g_a) * br_a + jax.nn.sigmoid(g_b) * br_b + jax.nn.sigmoid(g_m) * br_m
        x = x + dot(merged.astype(BF16), wo_ref[...])
        yield
        h2 = _rms(x, g2_ref[...]).astype(BF16)
        for c in range(D_FF // FF_CHUNK):
            u = dot(h2, wu_ref[:, c * FF_CHUNK:(c + 1) * FF_CHUNK])
            yield
            a = jnp.square(jnp.maximum(u, 0.0)).astype(BF16)
            x = x + dot(a, wd_ref[c * FF_CHUNK:(c + 1) * FF_CHUNK, :])
            yield
        if final_norm:
            x = _rms(x, gf_ref[...])
        y_ref[rows] = x

    done = object()
    waiting = [chain(j) for j in range(TAIL_CHAINS)]
    live, step = [], 0
    while live or waiting:
        if waiting and step % TAIL_SKEW == 0:
            live.append(waiting.pop(0))
        live = [g for g in live if next(g, done) is not done]
        step += 1


def _tail(x2d, oa, om, obs, lbs, g1, wg, wa, wb, wm, wo, g2, wu, wd, gf, tm, final_norm):
    t = x2d.shape[0]
    npair = HB_PER_GROUP // 2
    nt = obs[0].shape[2] * obs[0].shape[3] // tm

    def rows(width):
        return pl.BlockSpec((tm, width), lambda i: (i, 0))

    def whole(a):
        return pl.BlockSpec(a.shape, lambda i: (0, 0), pipeline_mode=pl.Buffered(1))

    def slab(a):
        dil = a.shape[2]
        return pl.BlockSpec((None, npair, dil, tm // dil, LANES), lambda i: (i // nt, 0, 0, i % nt, 0))

    consts = (g1, wg, wa, wb, wm, wo, g2, wu, wd, gf)
    return pl.pallas_call(
        functools.partial(_tail_kernel, final_norm=final_norm),
        grid=(t // tm,),
        in_specs=([rows(D_MODEL), rows(W_A), rows(W_M)] + [slab(a) for a in (*obs, *lbs)]
                  + [whole(a) for a in consts]),
        out_specs=rows(D_MODEL),
        out_shape=jax.ShapeDtypeStruct((t, D_MODEL), F32),
        scratch_shapes=[pltpu.VMEM((4, npair, tm, LANES), F32)],
        compiler_params=_params(("parallel",)),
        name="tail",
    )(x2d, oa, om, *obs, *lbs, *consts)


def _layer(x, mem, rel_bias, norm1_g, w_in, mem_norm_g, w_mem_kv, sink_logit, w_branch_a,
           w_branch_b, w_branch_m, w_out, norm2_g, w_up, w_down, final_norm_g, final_norm):
    b, seq, _ = x.shape
    n_mem = mem.shape[1]
    t = b * seq
    x2d = x.reshape(t, D_MODEL)

    off = np.cumsum((0,) + IN_SPLITS)
    half_heads = N_HEADS_A // 2
    scale = HEAD_DIM ** -0.5 * LOG2E
    w_qa = (w_in[:, off[0]:off[1]].reshape(D_MODEL, 2, half_heads, HEAD_DIM)
            .transpose(0, 2, 1, 3).reshape(D_MODEL, W_A)) * scale
    parts = [w_qa, w_in[:, off[1]:off[3]]]
    for g in range(len(DILATED_GROUPS)):
        lo, hi = g * W_B_OUT, (g + 1) * W_B_OUT
        parts += [w_in[:, off[3] + lo:off[3] + hi] * scale, w_in[:, off[4] + lo:off[4] + hi],
                  w_in[:, off[5] + lo:off[5] + hi]]
    parts.append(w_in[:, off[6]:off[7]] * (HEAD_DIM_M ** -0.5 * LOG2E))
    w_qkv = jnp.concatenate(parts, axis=1).astype(BF16)
    plan = ((W_A, 1), (2 * W_KV_A, 1)) + tuple((3 * W_B_OUT, dil) for _, dil in DILATED_GROUPS) + ((W_M, 1),)
    qa, kva, qkv0, qkv1, qkv2, qm = _in_proj(x, norm1_g.reshape(1, D_MODEL), w_qkv, plan, tm=PROJ_ROWS)

    bias_a = _bias_tiles(rel_bias, np.arange(N_HEADS_A), WIN_A, 1, 3 * WIN_A, (0, -WIN_A, -2 * WIN_A))

    obs, lbs = [], []
    for g, ((win, dil), qkv) in enumerate(zip(DILATED_GROUPS, (qkv0, qkv1, qkv2))):
        half = win // (2 * dil)
        length = seq // dil
        kw = min(4 * half, length)
        deltas = (0, -half, -2 * half) if length > kw else (0,)
        hcols = N_HEADS_A + g * HB_PER_GROUP + np.arange(HB_PER_GROUP)
        bias_g = _bias_tiles(rel_bias, hcols, half, dil, kw, deltas)
        o, l = _attn_dil(qkv.reshape(b, dil, length, 3 * W_B_OUT), bias_g, dil, half)
        obs.append(o)
        lbs.append(l)

    mkv, = _in_proj(mem.reshape(1, b * n_mem, D_MODEL), mem_norm_g.reshape(1, D_MODEL),
                    w_mem_kv.astype(BF16), ((2 * W_M, 1),), tm=CHAIN_ROWS)
    oa, om = _attn_win_mem(sink_logit * LOG2E, qa, kva, bias_a, qm, mkv, b, seq, n_mem)

    wa = (w_branch_a.reshape(2, half_heads, HEAD_DIM, D_MODEL).transpose(1, 0, 2, 3)
          .reshape(W_A, D_MODEL))
    y = _tail(x2d, oa, om, obs, lbs, norm1_g.reshape(1, D_MODEL), w_in[:, off[7]:].astype(BF16),
              wa.astype(BF16), w_branch_b.astype(BF16), w_branch_m.astype(BF16), w_out.astype(BF16),
              norm2_g.reshape(1, D_MODEL), w_up.astype(BF16), w_down.astype(BF16),
              final_norm_g.reshape(1, D_MODEL), tm=TAIL_ROWS, final_norm=final_norm)
    return y.reshape(b, seq, D_MODEL)


def _trunk(x, mem, rel_bias, norm1_g, w_in, mem_norm_g, w_mem_kv, sink_logit, w_branch_a,
           w_branch_b, w_branch_m, w_out, norm2_g, w_up, w_down, final_norm_g):
    depth = w_in.shape[0]
    for layer in range(depth):
        x = _layer(x, mem, rel_bias, norm1_g[layer], w_in[layer], mem_norm_g[layer], w_mem_kv[layer],
                   sink_logit[layer], w_branch_a[layer], w_branch_b[layer], w_branch_m[layer],
                   w_out[layer], norm2_g[layer], w_up[layer], w_down[layer], final_norm_g,
                   final_norm=layer == depth - 1)
    return x


def kernel(x_prompt, x_sample, mem_prompt, mem_sample, rel_bias, norm1_g, w_in, mem_norm_g, w_mem_kv,
           sink_logit, w_branch_a, w_branch_b, w_branch_m, w_out, norm2_g, w_up, w_down, final_norm_g):
    weights = (rel_bias, norm1_g, w_in, mem_norm_g, w_mem_kv, sink_logit, w_branch_a, w_branch_b,
               w_branch_m, w_out, norm2_g, w_up, w_down, final_norm_g)
    return (_trunk(x_prompt, mem_prompt, *weights), _trunk(x_sample, mem_sample, *weights))
```
